```python
import jax, jax.numpy as jnp
from jax import lax
import numpy as np

D_MODEL = 2048
BATCH = 4
SEQ = 2048
DEPTH = 1
DEC_BATCH = 32
DEC_SEQ = 8
PAST_LEN = 16384
PAGE_SIZE = 128

D_CONV = D_MODEL // 2
CONV_W = 3
N_HEADS = 16
HEAD_DIM = 64
D_ATTN = N_HEADS * HEAD_DIM
N_IDX_HEADS = 16
IDX_DIM = 64
TOPK_MAX = 256
D_FF = 4 * D_MODEL
Q_BLOCK = 128
EPS = 1e-6
IN_SIZES = (D_CONV, D_CONV, D_CONV, D_ATTN, D_ATTN, D_ATTN,
            N_IDX_HEADS * IDX_DIM, IDX_DIM, N_IDX_HEADS, D_MODEL, D_MODEL)
N_IN = sum(IN_SIZES)

kernel_name = "hybrid_conv_dsa_adaln_decode_step"


def _rmsnorm(x, g):
    xf = x.astype(jnp.float32)
    y = xf * lax.rsqrt(jnp.mean(xf * xf, axis=-1, keepdims=True) + EPS) * g.astype(jnp.float32)
    return y.astype(x.dtype)


def _split_in(z):
    offs = np.cumsum(np.array(IN_SIZES))[:-1].tolist()
    return jnp.split(z, offs, axis=-1)


def _index_scores(q_idx, w_idx, k_idx):
    dots = jnp.einsum('bthd,bsd->bths', q_idx, k_idx).astype(jnp.float32)
    return jnp.einsum('bths,bth->bts', jax.nn.relu(dots), w_idx.astype(jnp.float32))


def _attend_selected(q, k_sel, v_sel, valid):
    s = jnp.einsum('bthd,btkhd->bthk', q, k_sel).astype(jnp.float32) * (HEAD_DIM ** -0.5)
    s = jnp.where(valid[:, :, None, :], s, -jnp.inf)
    p = jax.nn.softmax(s, axis=-1).astype(v_sel.dtype)
    return jnp.einsum('bthk,btkhd->bthd', p, v_sel)


def _prompt_attention(q, k, v, qi, ki, wi):
    B, L, H, D = q.shape
    topk = min(TOPK_MAX, L // 4)
    nb = L // Q_BLOCK
    key_pos = jnp.arange(L)
    bidx = jnp.arange(B)[:, None, None]

    def blocks(a):
        return jnp.moveaxis(a.reshape((B, nb, Q_BLOCK) + a.shape[2:]), 1, 0)

    def one(args):
        q_b, qi_b, wi_b, j = args
        qpos = j * Q_BLOCK + jnp.arange(Q_BLOCK)
        sc = _index_scores(qi_b, wi_b, ki)
        sc = jnp.where(key_pos[None, None, :] <= qpos[None, :, None], sc, -jnp.inf)
        _, idx = lax.top_k(sc, topk)
        valid = idx <= qpos[None, :, None]
        return _attend_selected(q_b, k[bidx, idx], v[bidx, idx], valid)

    out = lax.map(one, (blocks(q), blocks(qi), blocks(wi), jnp.arange(nb)))
    return jnp.moveaxis(out, 0, 1).reshape(B, L, H, D)


def _sample_attention(q, k, v, qi, ki, wi, cache_k, cache_v, cache_k_idx, page_table):
    B, T, H, D = q.shape
    n_pages = page_table.shape[1]
    P = n_pages * PAGE_SIZE
    L = P + T
    topk = min(TOPK_MAX, L // 4)
    ki_past = cache_k_idx[page_table].reshape(B, P, IDX_DIM).astype(ki.dtype)
    ki_all = jnp.concatenate([ki_past, ki], axis=1)
    qpos = P + jnp.arange(T)
    key_pos = jnp.arange(L)
    sc = _index_scores(qi, wi, ki_all)
    sc = jnp.where(key_pos[None, None, :] <= qpos[None, :, None], sc, -jnp.inf)
    _, idx = lax.top_k(sc, topk)
    valid = idx <= qpos[None, :, None]
    bidx = jnp.arange(B)[:, None, None]
    pidx = jnp.minimum(idx, P - 1)
    phys = page_table[bidx, pidx // PAGE_SIZE]
    off = pidx % PAGE_SIZE
    nidx = jnp.clip(idx - P, 0, T - 1)
    in_past = (idx < P)[..., None, None]
    k_sel = jnp.where(in_past, cache_k[phys, off].astype(k.dtype), k[bidx, nidx])
    v_sel = jnp.where(in_past, cache_v[phys, off].astype(v.dtype), v[bidx, nidx])
    return _attend_selected(q, k_sel, v_sel, valid)


def _block(x, c, conv_prev, attn_fn, w_ada, b_ada, norm1_g, norm2_g, w_in, conv_w,
           idx_k_g, w_bc, w_ba, w_o, w_ff1, w_ff2):
    B, L, _ = x.shape
    mod = (jax.nn.silu(c) @ w_ada + b_ada)[:, None, :]
    sh1, sc1, g1, sh2, sc2, g2 = jnp.split(mod, 6, axis=-1)
    h = _rmsnorm(x, norm1_g) * (1 + sc1) + sh1
    u, gb, gc, q, k, v, qi, ki, wi, g_conv, g_attn = _split_in(h @ w_in)
    u = gc * u
    u_ext = jnp.concatenate([conv_prev.astype(u.dtype), u], axis=1)
    conv = conv_w[0] * u_ext[:, 0:L]
    for j in range(1, CONV_W):
        conv = conv + conv_w[j] * u_ext[:, j:j + L]
    y_conv = gb * conv
    new_conv = u_ext[:, -(CONV_W - 1):]
    q = q.reshape(B, L, N_HEADS, HEAD_DIM)
    k = k.reshape(B, L, N_HEADS, HEAD_DIM)
    v = v.reshape(B, L, N_HEADS, HEAD_DIM)
    qi = qi.reshape(B, L, N_IDX_HEADS, IDX_DIM)
    ki = _rmsnorm(ki, idx_k_g)
    wi = wi * (N_IDX_HEADS ** -0.5 * IDX_DIM ** -0.5)
    y_attn = attn_fn(q, k, v, qi, ki, wi).reshape(B, L, D_ATTN)
    m = jax.nn.sigmoid(g_conv) * (y_conv @ w_bc) + jax.nn.sigmoid(g_attn) * (y_attn @ w_ba)
    x = x + g1 * (m @ w_o)
    h2 = _rmsnorm(x, norm2_g) * (1 + sc2) + sh2
    x = x + g2 * (jnp.square(jax.nn.relu(h2 @ w_ff1)) @ w_ff2)
    return x, k, v, ki, new_conv


def setup_inputs(seed: int = 0) -> dict:
    key = jax.random.key(seed)
    ks = jax.random.split(key, 24)
    n_pages = PAST_LEN // PAGE_SIZE
    n_used = DEC_BATCH * n_pages
    n_pool = (n_used * 5) // 4
    f = jnp.float32
    nrm = lambda k, s, sc: jax.random.normal(k, s, f) * sc
    page_table = jax.random.permutation(ks[0], n_pool)[:n_used].reshape(DEC_BATCH, n_pages).astype(jnp.int32)
    return {
        "x_prompt": nrm(ks[1], (BATCH, SEQ, D_MODEL), 1.0),
        "x_sample": nrm(ks[2], (DEC_BATCH, DEC_SEQ, D_MODEL), 1.0),
        "cache_k": nrm(ks[3], (DEPTH, n_pool, PAGE_SIZE, N_HEADS, HEAD_DIM), 1.0),
        "cache_v": nrm(ks[4], (DEPTH, n_pool, PAGE_SIZE, N_HEADS, HEAD_DIM), 1.0),
        "cache_k_idx": nrm(ks[5], (DEPTH, n_pool, PAGE_SIZE, IDX_DIM), 1.0),
        "state_conv": nrm(ks[6], (DEPTH, DEC_BATCH, CONV_W - 1, D_CONV), 0.5),
        "page_table": page_table,
        "c_prompt": nrm(ks[7], (BATCH, D_MODEL), 1.0),
        "c_sample": nrm(ks[8], (DEC_BATCH, D_MODEL), 1.0),
        "w_ada": nrm(ks[9], (DEPTH, D_MODEL, 6 * D_MODEL), 0.5 * D_MODEL ** -0.5),
        "b_ada": nrm(ks[10], (DEPTH, 6 * D_MODEL), 0.02),
        "norm1_g": 1.0 + nrm(ks[11], (DEPTH, D_MODEL), 0.02),
        "norm2_g": 1.0 + nrm(ks[12], (DEPTH, D_MODEL), 0.02),
        "w_in": nrm(ks[13], (DEPTH, D_MODEL, N_IN), D_MODEL ** -0.5),
        "conv_w": nrm(ks[14], (DEPTH, CONV_W, D_CONV), CONV_W ** -0.5),
        "idx_k_g": 1.0 + nrm(ks[15], (DEPTH, IDX_DIM), 0.02),
        "w_bc": nrm(ks[16], (DEPTH, D_CONV, D_MODEL), D_CONV ** -0.5),
        "w_ba": nrm(ks[17], (DEPTH, D_ATTN, D_MODEL), D_ATTN ** -0.5),
        "w_o": nrm(ks[18], (DEPTH, D_MODEL, D_MODEL), D_MODEL ** -0.5),
        "w_ff1": nrm(ks[19], (DEPTH, D_MODEL, D_FF), D_MODEL ** -0.5),
        "w_ff2": nrm(ks[20], (DEPTH, D_FF, D_MODEL), D_FF ** -0.5),
        "final_g": 1.0 + nrm(ks[21], (D_MODEL,), 0.02),
    }


def reference(x_prompt, x_sample, cache_k, cache_v, cache_k_idx, state_conv, page_table,
              c_prompt, c_sample, w_ada, b_ada, norm1_g, norm2_g, w_in, conv_w, idx_k_g,
              w_bc, w_ba, w_o, w_ff1, w_ff2, final_g):
    xp, xs = x_prompt, x_sample
    kp_l, vp_l, kip_l, cvp_l = [], [], [], []
    ks_l, vs_l, kis_l, cvs_l = [], [], [], []
    for l in range(DEPTH):
        wl = (w_ada[l], b_ada[l], norm1_g[l], norm2_g[l], w_in[l], conv_w[l], idx_k_g[l],
              w_bc[l], w_ba[l], w_o[l], w_ff1[l], w_ff2[l])
        conv0 = jnp.zeros((xp.shape[0], CONV_W - 1, D_CONV), xp.dtype)
        xp, kp, vp, kip, cvp = _block(xp, c_prompt, conv0, _prompt_attention, *wl)
        attn_s = (lambda q, k, v, qi, ki, wi, ck=cache_k[l], cv=cache_v[l], cki=cache_k_idx[l]:
                  _sample_attention(q, k, v, qi, ki, wi, ck, cv, cki, page_table))
        xs, ks_, vs_, kis, cvs = _block(xs, c_sample, state_conv[l], attn_s, *wl)
        kp_l.append(kp); vp_l.append(vp); kip_l.append(kip); cvp_l.append(cvp)
        ks_l.append(ks_); vs_l.append(vs_); kis_l.append(kis); cvs_l.append(cvs)
    y_prompt = _rmsnorm(xp, final_g)
    y_sample = _rmsnorm(xs, final_g)
    return (y_prompt, y_sample,
            jnp.stack(kp_l), jnp.stack(vp_l), jnp.stack(kip_l), jnp.stack(cvp_l),
            jnp.stack(ks_l), jnp.stack(vs_l), jnp.stack(kis_l), jnp.stack(cvs_l))
```

```python
import functools

import jax
import jax.numpy as jnp
import numpy as np
from jax import lax
from jax.experimental import pallas as pl
from jax.experimental.pallas import tpu as pltpu

D_MODEL = 2048
PAGE_SIZE = 128
D_CONV = 1024
CONV_W = 3
N_HEADS = 16
HEAD_DIM = 64
D_ATTN = N_HEADS * HEAD_DIM
N_IDX_HEADS = 16
IDX_DIM = 64
TOPK_MAX = 256
D_FF = 4 * D_MODEL
EPS = 1e-6
IN_SIZES = (D_CONV, D_CONV, D_CONV, D_ATTN, D_ATTN, D_ATTN,
            N_IDX_HEADS * IDX_DIM, IDX_DIM, N_IDX_HEADS, D_MODEL, D_MODEL)
IN_OFFS = tuple(int(v) for v in np.cumsum((0,) + IN_SIZES))
OFF_U, OFF_GB, OFF_GC, OFF_Q, OFF_K, OFF_V, OFF_QI, OFF_KI, OFF_WI, OFF_GCONV, OFF_GATTN, N_IN = IN_OFFS

BF16 = jnp.bfloat16
F32 = jnp.float32
I32 = jnp.int32
INT_MIN = -2 ** 31
NEG_BIG = -1e30
VMEM_LIMIT_BYTES = 56 * 1024 * 1024
ATTN_BLOCK = 256


def _cparams(sem):
    return pltpu.CompilerParams(dimension_semantics=sem, vmem_limit_bytes=VMEM_LIMIT_BYTES)


def _mm_kernel(*refs, n_extra, n_out, trans_b, prologue, epilogue, nk):
    a_ref, b_ref = refs[0], refs[1]
    extra = refs[2:2 + n_extra]
    outs = refs[2 + n_extra:2 + n_extra + n_out]
    a = a_ref[...]
    if prologue is not None:
        a = prologue(a)
    a = a.astype(BF16)
    b = b_ref[...].astype(BF16)
    dims = (((1,), (1,)), ((), ())) if trans_b else (((1,), (0,)), ((), ()))
    d = lax.dot_general(a, b, dims, preferred_element_type=F32)

    def finish(acc):
        res = epilogue(acc, *[e[...] for e in extra])
        for o, r in zip(outs, res):
            o[...] = r.astype(o.dtype)

    if nk == 1:
        finish(d)
    else:
        acc_ref = refs[-1]
        k = pl.program_id(2)

        @pl.when(k == 0)
        def _():
            acc_ref[...] = d

        @pl.when(k > 0)
        def _():
            acc_ref[...] += d

        @pl.when(k == nk - 1)
        def _():
            finish(acc_ref[...])


def _mm(a, b, *, trans_b, tm, tn, tk=None, a_row_off=0, b_off=0, extras=(), outs, epilogue,
        prologue=None, m=None, n=None):
    kdim = a.shape[1]
    m = a.shape[0] if m is None else m
    if n is None:
        n = b.shape[0] if trans_b else b.shape[1]
    tk = kdim if tk is None else tk
    nk = kdim // tk
    grid = (pl.cdiv(m, tm), pl.cdiv(n, tn), nk)
    a_spec = pl.BlockSpec((tm, tk), lambda i, j, k: (i + a_row_off, k))
    if trans_b:
        b_spec = pl.BlockSpec((tn, tk), lambda i, j, k: (j + b_off, k))
    else:
        b_spec = pl.BlockSpec((tk, tn), lambda i, j, k: (k, j + b_off))
    kern = functools.partial(_mm_kernel, n_extra=len(extras), n_out=len(outs), trans_b=trans_b,
                             prologue=prologue, epilogue=epilogue, nk=nk)
    res = pl.pallas_call(
        kern,
        grid=grid,
        in_specs=[a_spec, b_spec] + [s for _, s in extras],
        out_specs=[s for _, s in outs],
        out_shape=[o for o, _ in outs],
        scratch_shapes=[pltpu.VMEM((tm, tn), F32)] if nk > 1 else [],
        compiler_params=_cparams(("parallel", "arbitrary", "arbitrary")),
    )(a, b, *[e for e, _ in extras])
    return res


def _spec_mn(tm, tn):
    return pl.BlockSpec((tm, tn), lambda i, j, k: (i, j))


def _spec_seq_vec(tm, tn, rows_per_seq):
    bps = rows_per_seq // tm
    return pl.BlockSpec((None, 1, tn), lambda i, j, k: (i // bps, 0, j))


def _norm_mod_kernel(x_ref, g_ref, sc_ref, sh_ref, o_ref):
    x = x_ref[...]
    r = lax.rsqrt(jnp.mean(x * x, axis=-1, keepdims=True) + EPS)
    o_ref[...] = ((x * r * g_ref[...]) * (1.0 + sc_ref[...]) + sh_ref[...]).astype(o_ref.dtype)


def _norm_mod(x, g, sc, sh, tm):
    n_seq, L, D = x.shape
    bps = L // tm
    return pl.pallas_call(
        _norm_mod_kernel,
        grid=(n_seq, bps),
        in_specs=[pl.BlockSpec((None, tm, D), lambda s, i: (s, i, 0)),
                  pl.BlockSpec((1, D), lambda s, i: (0, 0)),
                  pl.BlockSpec((None, 1, D), lambda s, i: (s, 0, 0)),
                  pl.BlockSpec((None, 1, D), lambda s, i: (s, 0, 0))],
        out_specs=pl.BlockSpec((tm, D), lambda s, i: (s * bps + i, 0)),
        out_shape=jax.ShapeDtypeStruct((n_seq * L, D), BF16),
        compiler_params=_cparams(("parallel", "arbitrary")),
    )(x, g.reshape(1, D), sc, sh)


def _conv_kernel(u_ref, gb_ref, gc_ref, prev_ref, w_ref, y_ref, new_ref, carry_ref):
    i = pl.program_id(1)
    tm = u_ref.shape[0]

    @pl.when(i == 0)
    def _():
        carry_ref[...] = prev_ref[...]

    cu = gc_ref[...] * u_ref[...]
    row = lax.broadcasted_iota(I32, cu.shape, 0)
    c0 = carry_ref[0:1, :]
    c1 = carry_ref[1:2, :]
    s1 = jnp.where(row == 0, c1, pltpu.roll(cu, 1, 0))
    s2 = jnp.where(row == 0, c0, jnp.where(row == 1, c1, pltpu.roll(cu, 2, 0)))
    conv = w_ref[0:1, :] * s2 + w_ref[1:2, :] * s1 + w_ref[2:3, :] * cu
    y_ref[...] = (gb_ref[...] * conv).astype(y_ref.dtype)
    last2 = cu[tm - 2:tm, :]
    carry_ref[...] = last2
    new_ref[...] = last2


def _gated_conv(z, prev, conv_w, n_seq, L, tm):
    C = D_CONV
    bps = L // tm
    nb = C // C
    row = lambda s, i: s * bps + i
    return pl.pallas_call(
        _conv_kernel,
        grid=(n_seq, bps),
        in_specs=[pl.BlockSpec((tm, C), lambda s, i: (row(s, i), 0)),
                  pl.BlockSpec((tm, C), lambda s, i: (row(s, i), nb)),
                  pl.BlockSpec((tm, C), lambda s, i: (row(s, i), 2 * nb)),
                  pl.BlockSpec((None, 2, C), lambda s, i: (s, 0, 0)),
                  pl.BlockSpec((CONV_W, C), lambda s, i: (0, 0))],
        out_specs=[pl.BlockSpec((tm, C), lambda s, i: (row(s, i), 0)),
                   pl.BlockSpec((None, 2, C), lambda s, i: (s, 0, 0))],
        out_shape=[jax.ShapeDtypeStruct((n_seq * L, C), BF16),
                   jax.ShapeDtypeStruct((n_seq, 2, C), F32)],
        scratch_shapes=[pltpu.VMEM((2, C), F32)],
        compiler_params=_cparams(("arbitrary", "arbitrary")),
    )(z, z, z, prev, conv_w)


def _merge_kernel(yc_ref, ya_ref, wbc_ref, wba_ref, gc_ref, ga_ref, o_ref):
    dc = jnp.dot(yc_ref[...], wbc_ref[...].astype(BF16), preferred_element_type=F32)
    da = jnp.dot(ya_ref[...], wba_ref[...].astype(BF16), preferred_element_type=F32)
    o_ref[...] = (jax.nn.sigmoid(gc_ref[...]) * dc + jax.nn.sigmoid(ga_ref[...]) * da).astype(o_ref.dtype)


def _merge(yc, ya, w_bc, w_ba, zg, tm, tn):
    M = yc.shape[0]
    nb = D_MODEL // tn
    return pl.pallas_call(
        _merge_kernel,
        grid=(M // tm, nb),
        in_specs=[pl.BlockSpec((tm, D_CONV), lambda i, j: (i, 0)),
                  pl.BlockSpec((tm, D_ATTN), lambda i, j: (i, 0)),
                  pl.BlockSpec((D_CONV, tn), lambda i, j: (0, j)),
                  pl.BlockSpec((D_ATTN, tn), lambda i, j: (0, j)),
                  pl.BlockSpec((tm, tn), lambda i, j: (i, j)),
                  pl.BlockSpec((tm, tn), lambda i, j: (i, j + nb))],
        out_specs=pl.BlockSpec((tm, tn), lambda i, j: (i, j)),
        out_shape=jax.ShapeDtypeStruct((M, D_MODEL), BF16),
        compiler_params=_cparams(("parallel", "arbitrary")),
    )(yc, ya, w_bc, w_ba, zg, zg)


def _sortable_key(x):
    bits = lax.bitcast_convert_type(x, I32)
    return bits ^ ((bits >> 31) & 0x7FFFFFFF)


def _kth_largest_key(count_ge, k, shape):
    base = jnp.where(count_ge(jnp.zeros(shape, I32)) >= k, 0, INT_MIN).astype(I32)

    def bit_body(t, base):
        cand = base + jnp.left_shift(jnp.int32(1), 30 - t)
        return jnp.where(count_ge(cand) >= k, cand, base)

    return lax.fori_loop(0, 31, bit_body, base)


def _prompt_attn_kernel(qT_ref, kT_ref, vT_ref, qiT_ref, ki_ref, wT_ref, o_ref,
                        key_scr, bias_scr, outT_scr, *, topk):
    j = pl.program_id(1)
    tq = qT_ref.shape[1]
    ck = tq
    sub = ck // 2

    def score_body(c, _):
        k0 = pl.multiple_of(c * sub, sub)
        ki_c = ki_ref[pl.ds(k0, sub), :].astype(BF16)
        acc = jnp.zeros((sub, tq), F32)
        for h in range(N_IDX_HEADS):
            d = jnp.dot(ki_c, qiT_ref[h * IDX_DIM:(h + 1) * IDX_DIM, :], preferred_element_type=F32)
            acc = acc + jnp.maximum(d, 0.0) * wT_ref[h:h + 1, :]
        kpos = k0 + lax.broadcasted_iota(I32, (sub, tq), 0)
        qpos = j * tq + lax.broadcasted_iota(I32, (sub, tq), 1)
        key_scr[pl.ds(k0, sub), :] = jnp.where(kpos <= qpos, _sortable_key(acc), INT_MIN)
        return 0

    lax.fori_loop(0, 2 * (j + 1), score_body, 0)

    def count_ge(cand):
        def body(c, cnt):
            blk = key_scr[pl.ds(pl.multiple_of(c * ck, ck), ck), :]
            return cnt + jnp.sum((blk >= cand).astype(I32), axis=0, keepdims=True)
        return lax.fori_loop(0, j + 1, body, jnp.zeros((1, tq), I32))

    thr = _kth_largest_key(count_ge, topk, (1, tq))
    thr = jnp.maximum(thr, INT_MIN + 1)

    def bias_body(c, _):
        k0 = pl.multiple_of(c * ck, ck)
        bias_scr[pl.ds(k0, ck), :] = jnp.where(key_scr[pl.ds(k0, ck), :] >= thr, 0.0, NEG_BIG)
        return 0

    lax.fori_loop(0, j + 1, bias_body, 0)

    def head_body(h, _):
        r0 = pl.multiple_of(h * HEAD_DIM, HEAD_DIM)
        qh = qT_ref[pl.ds(r0, HEAD_DIM), :]

        def chunk_body(c, carry):
            m, l, acc = carry
            k0 = pl.multiple_of(c * ck, ck)
            kc = kT_ref[pl.ds(r0, HEAD_DIM), pl.ds(k0, ck)]
            s = lax.dot_general(kc, qh, (((0,), (0,)), ((), ())), preferred_element_type=F32)
            s = s + bias_scr[pl.ds(k0, ck), :]
            m_new = jnp.maximum(m, jnp.max(s, axis=0, keepdims=True))
            p = jnp.exp(s - m_new)
            alpha = jnp.exp(m - m_new)
            l = l * alpha + jnp.sum(p, axis=0, keepdims=True)
            vc = vT_ref[pl.ds(r0, HEAD_DIM), pl.ds(k0, ck)]
            acc = acc * alpha + jnp.dot(vc, p.astype(BF16), preferred_element_type=F32)
            return m_new, l, acc

        m0 = jnp.full((1, tq), NEG_BIG, F32)
        l0 = jnp.zeros((1, tq), F32)
        a0 = jnp.zeros((HEAD_DIM, tq), F32)
        m, l, acc = lax.fori_loop(0, j + 1, chunk_body, (m0, l0, a0))
        outT_scr[pl.ds(r0, HEAD_DIM), :] = acc / l
        return 0

    lax.fori_loop(0, N_HEADS, head_body, 0)
    o_ref[...] = outT_scr[...].T.astype(o_ref.dtype)


def _prompt_attention(qT, kT, vT, qiT, ki, wT, topk):
    B, _, L = qT.shape
    tq = ATTN_BLOCK
    nq = L // tq
    feat = lambda b, j: (b, 0, j)
    whole = lambda b, j: (b, 0, 0)
    return pl.pallas_call(
        functools.partial(_prompt_attn_kernel, topk=topk),
        grid=(B, nq),
        in_specs=[pl.BlockSpec((None, D_ATTN, tq), feat),
                  pl.BlockSpec((None, D_ATTN, L), whole),
                  pl.BlockSpec((None, D_ATTN, L), whole),
                  pl.BlockSpec((None, N_IDX_HEADS * IDX_DIM, tq), feat),
                  pl.BlockSpec((None, L, IDX_DIM), whole),
                  pl.BlockSpec((None, N_IDX_HEADS, tq), feat)],
        out_specs=pl.BlockSpec((tq, D_ATTN), lambda b, j: (b * nq + j, 0)),
        out_shape=jax.ShapeDtypeStruct((B * L, D_ATTN), BF16),
        scratch_shapes=[pltpu.VMEM((L, tq), I32), pltpu.VMEM((L, tq), F32),
                        pltpu.VMEM((D_ATTN, tq), F32)],
        compiler_params=_cparams(("parallel", "arbitrary")),
    )(qT, kT, vT, qiT, ki, wT)


def _index_scores(q_idx, w_idx, k_idx):
    dots = jnp.einsum('bthd,bsd->bths', q_idx, k_idx).astype(jnp.float32)
    return jnp.einsum('bths,bth->bts', jax.nn.relu(dots), w_idx.astype(jnp.float32))


def _attend_selected(q, k_sel, v_sel, valid):
    s = jnp.einsum('bthd,btkhd->bthk', q, k_sel).astype(jnp.float32) * (HEAD_DIM ** -0.5)
    s = jnp.where(valid[:, :, None, :], s, -jnp.inf)
    p = jax.nn.softmax(s, axis=-1).astype(v_sel.dtype)
    return jnp.einsum('bthk,btkhd->bthd', p, v_sel)


def _sample_attention(q, k, v, qi, ki, wi, cache_k, cache_v, cache_k_idx, page_table):
    B, T, H, D = q.shape
    n_pages = page_table.shape[1]
    P = n_pages * PAGE_SIZE
    L = P + T
    topk = min(TOPK_MAX, L // 4)
    ki_past = cache_k_idx[page_table].reshape(B, P, IDX_DIM).astype(ki.dtype)
    ki_all = jnp.concatenate([ki_past, ki], axis=1)
    qpos = P + jnp.arange(T)
    key_pos = jnp.arange(L)
    sc = _index_scores(qi, wi, ki_all)
    sc = jnp.where(key_pos[None, None, :] <= qpos[None, :, None], sc, -jnp.inf)
    _, idx = lax.top_k(sc, topk)
    valid = idx <= qpos[None, :, None]
    bidx = jnp.arange(B)[:, None, None]
    pidx = jnp.minimum(idx, P - 1)
    phys = page_table[bidx, pidx // PAGE_SIZE]
    off = pidx % PAGE_SIZE
    nidx = jnp.clip(idx - P, 0, T - 1)
    in_past = (idx < P)[..., None, None]
    k_sel = jnp.where(in_past, cache_k[phys, off].astype(k.dtype), k[bidx, nidx])
    v_sel = jnp.where(in_past, cache_v[phys, off].astype(v.dtype), v[bidx, nidx])
    return _attend_selected(q, k_sel, v_sel, valid)


def _silu(x):
    return x * jax.nn.sigmoid(x)


def _adaln(c_all, w_ada, b_ada):
    n = c_all.shape[0]
    n_pad = -(-n // 8) * 8
    c_pad = jnp.pad(c_all, ((0, n_pad - n), (0, 0)))
    tn = 1536
    (mod,) = _mm(c_pad, w_ada, trans_b=False, tm=n_pad, tn=tn, prologue=_silu,
                 extras=[(b_ada.reshape(1, -1), pl.BlockSpec((1, tn), lambda i, j, k: (0, j)))],
                 outs=[(jax.ShapeDtypeStruct((n_pad, w_ada.shape[1]), F32), _spec_mn(n_pad, tn))],
                 epilogue=lambda acc, b: (acc + b,))
    return mod[:n]


def _ki_wi_epilogue(acc, g):
    ki = acc[:, :IDX_DIM]
    r = lax.rsqrt(jnp.mean(ki * ki, axis=-1, keepdims=True) + EPS)
    wi = acc[:, IDX_DIM:IDX_DIM + N_IDX_HEADS] * (N_IDX_HEADS ** -0.5 * IDX_DIM ** -0.5)
    return ki * r * g, wi


def _small_proj(h, wT_small, idx_k_g, tm):
    M = h.shape[0]
    n = IDX_DIM + N_IDX_HEADS
    return _mm(h, wT_small, trans_b=True, tm=tm, tn=n,
               extras=[(idx_k_g.reshape(1, IDX_DIM), pl.BlockSpec((1, IDX_DIM), lambda i, j, k: (0, 0)))],
               outs=[(jax.ShapeDtypeStruct((M, IDX_DIM), F32), pl.BlockSpec((tm, IDX_DIM), lambda i, j, k: (i, 0))),
                     (jax.ShapeDtypeStruct((M, N_IDX_HEADS), F32),
                      pl.BlockSpec((tm, N_IDX_HEADS), lambda i, j, k: (i, 0)))],
               epilogue=_ki_wi_epilogue)


def _proj_T(wT, row_off, n_rows, h, n_seq, L, scale, want_f32):
    tr, tc = 512, 1024
    cps = L // tc
    spec = pl.BlockSpec((None, tr, tc), lambda i, j, k: (j // cps, i, j % cps))
    outs = [(jax.ShapeDtypeStruct((n_seq, n_rows, L), BF16), spec)]
    if want_f32:
        outs.append((jax.ShapeDtypeStruct((n_seq, n_rows, L), F32), spec))
    ep = (lambda acc: (acc * scale,) * len(outs)) if scale != 1.0 else (lambda acc: (acc,) * len(outs))
    return _mm(wT, h, trans_b=True, tm=tr, tn=tc, a_row_off=row_off // tr, m=n_rows,
               outs=outs, epilogue=ep)


def _tail(x_flat, m, w_o, w_ff1, w_ff2, norm2_g, final_g, g1, sc2, sh2, g2, vec_spec, tm):
    M, D = x_flat.shape
    row_spec = pl.BlockSpec((tm, D), lambda i, j, k: (i, 0))
    gvec = pl.BlockSpec((1, D), lambda i, j, k: (0, 0))

    def ep_o(acc, x, g1_, n2g, sc2_, sh2_):
        x1 = x + g1_ * acc
        r = lax.rsqrt(jnp.mean(x1 * x1, axis=-1, keepdims=True) + EPS)
        return x1, (x1 * r * n2g) * (1.0 + sc2_) + sh2_

    x1, h2 = _mm(m, w_o, trans_b=False, tm=tm, tn=D, tk=512,
                 extras=[(x_flat, row_spec), (g1, vec_spec), (norm2_g.reshape(1, D), gvec),
                         (sc2, vec_spec), (sh2, vec_spec)],
                 outs=[(jax.ShapeDtypeStruct((M, D), F32), row_spec),
                       (jax.ShapeDtypeStruct((M, D), BF16), row_spec)],
                 epilogue=ep_o)

    tm1 = min(1024, M)
    (a,) = _mm(h2, w_ff1, trans_b=False, tm=tm1, tn=512,
               outs=[(jax.ShapeDtypeStruct((M, D_FF), BF16), _spec_mn(tm1, 512))],
               epilogue=lambda acc: (jnp.square(jnp.maximum(acc, 0.0)),))

    def ep_f(acc, x1_, g2_, fg):
        x2 = x1_ + g2_ * acc
        r = lax.rsqrt(jnp.mean(x2 * x2, axis=-1, keepdims=True) + EPS)
        return (x2 * r * fg,)

    (y,) = _mm(a, w_ff2, trans_b=False, tm=tm, tn=D, tk=512,
               extras=[(x1, row_spec), (g2, vec_spec), (final_g.reshape(1, D), gvec)],
               outs=[(jax.ShapeDtypeStruct((M, D), F32), row_spec)],
               epilogue=ep_f)
    return y


def kernel(x_prompt, x_sample, cache_k, cache_v, cache_k_idx, state_conv, page_table,
           c_prompt, c_sample, w_ada, b_ada, norm1_g, norm2_g, w_in, conv_w, idx_k_g,
           w_bc, w_ba, w_o, w_ff1, w_ff2, final_g):
    B, L, D = x_prompt.shape
    Bs, T, _ = x_sample.shape
    Mp, Ms = B * L, Bs * T
    l = 0
    wT = jnp.transpose(w_in[l])
    wT_gates = wT[OFF_GCONV:]
    wT_small = wT[OFF_KI:OFF_GCONV]

    mod = _adaln(jnp.concatenate([c_prompt, c_sample], axis=0), w_ada[l], b_ada[l])
    mod_p = [v[:, None, :] for v in jnp.split(mod[:B], 6, axis=-1)]
    mod_s = [v[:, None, :] for v in jnp.split(mod[B:], 6, axis=-1)]

    sh1, sc1, g1, sh2, sc2, g2 = mod_p
    tm = 1024
    h = _norm_mod(x_prompt, norm1_g[l], sc1, sh1, tm)
    (zA,) = _mm(h, wT, trans_b=True, tm=tm, tn=512, n=3 * D_CONV,
                outs=[(jax.ShapeDtypeStruct((Mp, 3 * D_CONV), F32), _spec_mn(tm, 512))],
                epilogue=lambda acc: (acc,))
    (zG,) = _mm(h, wT_gates, trans_b=True, tm=tm, tn=512,
                outs=[(jax.ShapeDtypeStruct((Mp, 2 * D), F32), _spec_mn(tm, 512))],
                epilogue=lambda acc: (acc,))
    ki_p, wi_p = _small_proj(h, wT_small, idx_k_g[l], tm)
    (qT,) = _proj_T(wT, OFF_Q, D_ATTN, h, B, L, HEAD_DIM ** -0.5, False)
    kT, kT32 = _proj_T(wT, OFF_K, D_ATTN, h, B, L, 1.0, True)
    vT, vT32 = _proj_T(wT, OFF_V, D_ATTN, h, B, L, 1.0, True)
    (qiT,) = _proj_T(wT, OFF_QI, N_IDX_HEADS * IDX_DIM, h, B, L, 1.0, False)

    conv0 = jnp.zeros((B, CONV_W - 1, D_CONV), F32)
    yc, cv_p = _gated_conv(zA, conv0, conv_w[l], B, L, tm)
    wT_p = jnp.transpose(wi_p.reshape(B, L, N_IDX_HEADS), (0, 2, 1))
    ya = _prompt_attention(qT, kT, vT, qiT, ki_p.reshape(B, L, IDX_DIM), wT_p, min(TOPK_MAX, L // 4))
    m = _merge(yc, ya, w_bc[l], w_ba[l], zG, tm, 512)
    seq_vec = _spec_seq_vec(512, D, L)
    y_prompt = _tail(x_prompt.reshape(Mp, D), m, w_o[l], w_ff1[l], w_ff2[l], norm2_g[l], final_g,
                     g1, sc2, sh2, g2, seq_vec, 512).reshape(B, L, D)
    k_prompt = jnp.transpose(kT32.reshape(B, N_HEADS, HEAD_DIM, L), (0, 3, 1, 2))
    v_prompt = jnp.transpose(vT32.reshape(B, N_HEADS, HEAD_DIM, L), (0, 3, 1, 2))
    kidx_prompt = ki_p.reshape(B, L, IDX_DIM)

    sh1, sc1, g1, sh2, sc2, g2 = mod_s
    hs = _norm_mod(x_sample, norm1_g[l], sc1, sh1, T)
    (zs,) = _mm(hs, wT, trans_b=True, tm=Ms, tn=512,
                outs=[(jax.ShapeDtypeStruct((Ms, N_IN), F32), _spec_mn(Ms, 512))],
                epilogue=lambda acc: (acc,))
    ki_s, wi_s = _small_proj(hs, wT_small, idx_k_g[l], Ms)
    ycs, cv_s = _gated_conv(zs, state_conv[l], conv_w[l], Bs, T, T)
    q_s = zs[:, OFF_Q:OFF_K].reshape(Bs, T, N_HEADS, HEAD_DIM)
    k_s = zs[:, OFF_K:OFF_V].reshape(Bs, T, N_HEADS, HEAD_DIM)
    v_s = zs[:, OFF_V:OFF_QI].reshape(Bs, T, N_HEADS, HEAD_DIM)
    qi_s = zs[:, OFF_QI:OFF_KI].reshape(Bs, T, N_IDX_HEADS, IDX_DIM)
    ya_s = _sample_attention(q_s, k_s, v_s, qi_s, ki_s.reshape(Bs, T, IDX_DIM),
                             wi_s.reshape(Bs, T, N_IDX_HEADS), cache_k[l], cache_v[l],
                             cache_k_idx[l], page_table).reshape(Ms, D_ATTN).astype(BF16)
    m_s = _merge(ycs, ya_s, w_bc[l], w_ba[l], zs[:, OFF_GCONV:], Ms, 512)
    tok = lambda v: jnp.broadcast_to(v, (Bs, T, D)).reshape(Ms, D)
    tok_spec = pl.BlockSpec((Ms, D), lambda i, j, k: (0, 0))
    y_sample = _tail(x_sample.reshape(Ms, D), m_s, w_o[l], w_ff1[l], w_ff2[l], norm2_g[l], final_g,
                     tok(g1), tok(sc2), tok(sh2), tok(g2), tok_spec, Ms).reshape(Bs, T, D)

    st = lambda a_: a_[None]
    return (y_prompt, y_sample, st(k_prompt), st(v_prompt), st(kidx_prompt), st(cv_p),
            st(k_s), st(v_s), st(ki_s.reshape(Bs, T, IDX_DIM)), st(cv_s))
```

```python
import functools

import jax
import jax.numpy as jnp
import numpy as np
from jax import lax
from jax.experimental import pallas as pl
from jax.experimental.pallas import tpu as pltpu

D_MODEL = 2048
PAGE_SIZE = 128
D_CONV = 1024
CONV_W = 3
N_HEADS = 16
HEAD_DIM = 64
D_ATTN = N_HEADS * HEAD_DIM
N_IDX_HEADS = 16
IDX_DIM = 64
TOPK_MAX = 256
D_FF = 4 * D_MODEL
EPS = 1e-6
IN_SIZES = (D_CONV, D_CONV, D_CONV, D_ATTN, D_ATTN, D_ATTN,
            N_IDX_HEADS * IDX_DIM, IDX_DIM, N_IDX_HEADS, D_MODEL, D_MODEL)
IN_OFFS = tuple(int(v) for v in np.cumsum((0,) + IN_SIZES))
OFF_U, OFF_GB, OFF_GC, OFF_Q, OFF_K, OFF_V, OFF_QI, OFF_KI, OFF_WI, OFF_GCONV, OFF_GATTN, N_IN = IN_OFFS

BF16 = jnp.bfloat16
F32 = jnp.float32
I32 = jnp.int32
INT_MIN = -2 ** 31
NEG_BIG = -1e30
VMEM_LIMIT_BYTES = 56 * 1024 * 1024
ATTN_BLOCK = 256


def _cparams(sem):
    return pltpu.CompilerParams(dimension_semantics=sem, vmem_limit_bytes=VMEM_LIMIT_BYTES)


def _mm_kernel(a_ref, b_ref, *refs, n_extra, trans_b, prologue, epilogue):
    extra, outs = refs[:n_extra], refs[n_extra:]
    a = a_ref[...]
    if prologue is not None:
        a = prologue(a)
    dims = (((1,), (1,)), ((), ())) if trans_b else (((1,), (0,)), ((), ()))
    d = lax.dot_general(a.astype(BF16), b_ref[...].astype(BF16), dims, preferred_element_type=F32)
    for o, r in zip(outs, epilogue(d, *[e[...] for e in extra])):
        o[...] = r.astype(o.dtype)


def _mm(a, b, *, trans_b, tm, tn, a_row_off=0, extras=(), outs, epilogue, prologue=None, m=None, n=None):
    kdim = a.shape[1]
    m = a.shape[0] if m is None else m
    if n is None:
        n = b.shape[0] if trans_b else b.shape[1]
    b_spec = pl.BlockSpec((tn, kdim), lambda i, j: (j, 0)) if trans_b else pl.BlockSpec((kdim, tn), lambda i, j: (0, j))
    return pl.pallas_call(
        functools.partial(_mm_kernel, n_extra=len(extras), trans_b=trans_b, prologue=prologue, epilogue=epilogue),
        grid=(pl.cdiv(m, tm), pl.cdiv(n, tn)),
        in_specs=[pl.BlockSpec((tm, kdim), lambda i, j: (i + a_row_off, 0)), b_spec] + [s for _, s in extras],
        out_specs=[s for _, s in outs],
        out_shape=[o for o, _ in outs],
        compiler_params=_cparams(("parallel", "arbitrary")),
    )(a, b, *[e for e, _ in extras])


def _mm_rows_kernel(a_ref, b_ref, *refs, n_step, n_fin, step_fn, final_fn):
    steps, fins, outs = refs[:n_step], refs[n_step:n_step + n_fin], refs[n_step + n_fin:]
    j = pl.program_id(1)
    tn = b_ref.shape[1]
    d = jnp.dot(a_ref[...], b_ref[...].astype(BF16), preferred_element_type=F32)
    outs[0][:, pl.ds(pl.multiple_of(j * tn, tn), tn)] = step_fn(d, *[e[...] for e in steps])

    @pl.when(j == pl.num_programs(1) - 1)
    def _():
        for o, r in zip(outs, final_fn(outs[0][...], *[f[...] for f in fins])):
            if r is not None:
                o[...] = r.astype(o.dtype)


def _mm_rows(a, b, *, tm, tn, step_extras, fin_extras, out_dtypes, step_fn, final_fn):
    M, kdim = a.shape
    N = b.shape[1]
    row = pl.BlockSpec((tm, N), lambda i, j: (i, 0))
    return pl.pallas_call(
        functools.partial(_mm_rows_kernel, n_step=len(step_extras), n_fin=len(fin_extras),
                          step_fn=step_fn, final_fn=final_fn),
        grid=(M // tm, N // tn),
        in_specs=[pl.BlockSpec((tm, kdim), lambda i, j: (i, 0)), pl.BlockSpec((kdim, tn), lambda i, j: (0, j))]
        + [s for _, s in step_extras] + [s for _, s in fin_extras],
        out_specs=[row] * len(out_dtypes),
        out_shape=[jax.ShapeDtypeStruct((M, N), dt) for dt in out_dtypes],
        compiler_params=_cparams(("parallel", "arbitrary")),
    )(a, b, *[e for e, _ in step_extras], *[e for e, _ in fin_extras])


def _spec_mn(tm, tn):
    return pl.BlockSpec((tm, tn), lambda i, j: (i, j))


class _RowVec:
    def __init__(self, tm, rows_per_seq):
        self.tm, self.bps = tm, (None if rows_per_seq is None else rows_per_seq // tm)

    def cols(self, tn):
        if self.bps is None:
            return pl.BlockSpec((self.tm, tn), lambda i, j: (i, j))
        return pl.BlockSpec((None, 1, tn), lambda i, j: (i // self.bps, 0, j))

    def full(self, n):
        if self.bps is None:
            return pl.BlockSpec((self.tm, n), lambda i, j: (i, 0))
        return pl.BlockSpec((None, 1, n), lambda i, j: (i // self.bps, 0, 0))


def _norm_mod_kernel(x_ref, g_ref, sc_ref, sh_ref, o_ref):
    x = x_ref[...]
    r = lax.rsqrt(jnp.mean(x * x, axis=-1, keepdims=True) + EPS)
    o_ref[...] = ((x * r * g_ref[...]) * (1.0 + sc_ref[...]) + sh_ref[...]).astype(o_ref.dtype)


def _norm_mod(x, g, sc, sh, tm):
    n_seq, L, D = x.shape
    bps = L // tm
    return pl.pallas_call(
        _norm_mod_kernel,
        grid=(n_seq, bps),
        in_specs=[pl.BlockSpec((None, tm, D), lambda s, i: (s, i, 0)),
                  pl.BlockSpec((1, D), lambda s, i: (0, 0)),
                  pl.BlockSpec((None, 1, D), lambda s, i: (s, 0, 0)),
                  pl.BlockSpec((None, 1, D), lambda s, i: (s, 0, 0))],
        out_specs=pl.BlockSpec((tm, D), lambda s, i: (s * bps + i, 0)),
        out_shape=jax.ShapeDtypeStruct((n_seq * L, D), BF16),
        compiler_params=_cparams(("parallel", "arbitrary")),
    )(x, g.reshape(1, D), sc, sh)


def _conv_kernel(u_ref, gb_ref, gc_ref, prev_ref, w_ref, y_ref, new_ref, carry_ref):
    i = pl.program_id(1)
    tm = u_ref.shape[0]

    @pl.when(i == 0)
    def _():
        carry_ref[...] = prev_ref[...]

    cu = gc_ref[...] * u_ref[...]
    row = lax.broadcasted_iota(I32, cu.shape, 0)
    c0 = carry_ref[0:1, :]
    c1 = carry_ref[1:2, :]
    s1 = jnp.where(row == 0, c1, pltpu.roll(cu, 1, 0))
    s2 = jnp.where(row == 0, c0, jnp.where(row == 1, c1, pltpu.roll(cu, 2, 0)))
    conv = w_ref[0:1, :] * s2 + w_ref[1:2, :] * s1 + w_ref[2:3, :] * cu
    y_ref[...] = (gb_ref[...] * conv).astype(y_ref.dtype)
    last2 = cu[tm - 2:tm, :]
    carry_ref[...] = last2
    new_ref[...] = last2


def _gated_conv(z, prev, conv_w, n_seq, L, tm):
    C = D_CONV
    bps = L // tm
    row = lambda s, i: s * bps + i
    return pl.pallas_call(
        _conv_kernel,
        grid=(n_seq, bps),
        in_specs=[pl.BlockSpec((tm, C), lambda s, i: (row(s, i), 0)),
                  pl.BlockSpec((tm, C), lambda s, i: (row(s, i), 1)),
                  pl.BlockSpec((tm, C), lambda s, i: (row(s, i), 2)),
                  pl.BlockSpec((None, 2, C), lambda s, i: (s, 0, 0)),
                  pl.BlockSpec((CONV_W, C), lambda s, i: (0, 0))],
        out_specs=[pl.BlockSpec((tm, C), lambda s, i: (row(s, i), 0)),
                   pl.BlockSpec((None, 2, C), lambda s, i: (s, 0, 0))],
        out_shape=[jax.ShapeDtypeStruct((n_seq * L, C), BF16),
                   jax.ShapeDtypeStruct((n_seq, 2, C), F32)],
        scratch_shapes=[pltpu.VMEM((2, C), F32)],
        compiler_params=_cparams(("arbitrary", "arbitrary")),
    )(z, z, z, prev, conv_w)


def _merge_kernel(yc_ref, ya_ref, wbc_ref, wba_ref, gc_ref, ga_ref, o_ref):
    dc = jnp.dot(yc_ref[...], wbc_ref[...].astype(BF16), preferred_element_type=F32)
    da = jnp.dot(ya_ref[...], wba_ref[...].astype(BF16), preferred_element_type=F32)
    o_ref[...] = (jax.nn.sigmoid(gc_ref[...]) * dc + jax.nn.sigmoid(ga_ref[...]) * da).astype(o_ref.dtype)


def _merge(yc, ya, w_bc, w_ba, zg, tm, tn):
    M = yc.shape[0]
    nb = D_MODEL // tn
    return pl.pallas_call(
        _merge_kernel,
        grid=(M // tm, nb),
        in_specs=[pl.BlockSpec((tm, D_CONV), lambda i, j: (i, 0)),
                  pl.BlockSpec((tm, D_ATTN), lambda i, j: (i, 0)),
                  pl.BlockSpec((D_CONV, tn), lambda i, j: (0, j)),
                  pl.BlockSpec((D_ATTN, tn), lambda i, j: (0, j)),
                  pl.BlockSpec((tm, tn), lambda i, j: (i, j)),
                  pl.BlockSpec((tm, tn), lambda i, j: (i, j + nb))],
        out_specs=pl.BlockSpec((tm, tn), lambda i, j: (i, j)),
        out_shape=jax.ShapeDtypeStruct((M, D_MODEL), BF16),
        compiler_params=_cparams(("parallel", "arbitrary")),
    )(yc, ya, w_bc, w_ba, zg, zg)


def _sortable_key(x):
    bits = lax.bitcast_convert_type(x, I32)
    return bits ^ ((bits >> 31) & 0x7FFFFFFF)


def _kth_largest_key(count_ge, k, shape):
    base = jnp.where(count_ge(jnp.zeros(shape, I32)) >= k, 0, INT_MIN).astype(I32)

    def bit_body(t, base):
        cand = base + jnp.left_shift(jnp.int32(1), 30 - t)
        return jnp.where(count_ge(cand) >= k, cand, base)

    return lax.fori_loop(0, 31, bit_body, base)


def _prompt_attn_kernel(qT_ref, kT_ref, vT_ref, qiT_ref, ki_ref, wT_ref, o_ref,
                        key_scr, bias_scr, outT_scr, m_scr, l_scr, s_scr, *, topk):
    j = pl.program_id(1)
    tq = qT_ref.shape[1]
    ck = tq
    sub = ck // 2

    def score_body(c, _):
        k0 = pl.multiple_of(c * sub, sub)
        ki_c = ki_ref[pl.ds(k0, sub), :].astype(BF16)
        acc = jnp.zeros((sub, tq), F32)
        for h in range(N_IDX_HEADS):
            d = jnp.dot(ki_c, qiT_ref[h * IDX_DIM:(h + 1) * IDX_DIM, :], preferred_element_type=F32)
            acc = acc + jnp.maximum(d, 0.0) * wT_ref[h:h + 1, :]
        kpos = k0 + lax.broadcasted_iota(I32, (sub, tq), 0)
        qpos = j * tq + lax.broadcasted_iota(I32, (sub, tq), 1)
        key_scr[pl.ds(k0, sub), :] = jnp.where(kpos <= qpos, _sortable_key(acc), INT_MIN)
        return 0

    lax.fori_loop(0, 2 * (j + 1), score_body, 0)

    def count_ge(cand):
        def body(c, cnt):
            blk = key_scr[pl.ds(pl.multiple_of(c * ck, ck), ck), :]
            return cnt + jnp.sum((blk >= cand).astype(I32), axis=0, keepdims=True)
        return lax.fori_loop(0, j + 1, body, jnp.zeros((1, tq), I32))

    thr = _kth_largest_key(count_ge, topk, (1, tq))
    thr = jnp.maximum(thr, INT_MIN + 1)

    def bias_body(c, _):
        k0 = pl.multiple_of(c * ck, ck)
        bias_scr[pl.ds(k0, ck), :] = jnp.where(key_scr[pl.ds(k0, ck), :] >= thr, 0.0, NEG_BIG)
        return 0

    lax.fori_loop(0, j + 1, bias_body, 0)

    m_scr[...] = jnp.full(m_scr.shape, NEG_BIG, F32)
    l_scr[...] = jnp.zeros(l_scr.shape, F32)
    outT_scr[...] = jnp.zeros(outT_scr.shape, F32)

    def chunk_body(c, _):
        k0 = pl.multiple_of(c * ck, ck)

        def logits(h):
            r = slice(h * HEAD_DIM, (h + 1) * HEAD_DIM)
            return lax.dot_general(kT_ref[r, pl.ds(k0, ck)], qT_ref[r, :], (((0,), (0,)), ((), ())),
                                   preferred_element_type=F32)

        s_scr[0] = logits(0)
        for h in range(N_HEADS):
            r = slice(h * HEAD_DIM, (h + 1) * HEAD_DIM)
            if h + 1 < N_HEADS:
                s_scr[(h + 1) % 2] = logits(h + 1)
            s = s_scr[h % 2] + bias_scr[pl.ds(k0, ck), :]
            m_old = m_scr[h:h + 1, :]
            m_new = jnp.maximum(m_old, jnp.max(s, axis=0, keepdims=True))
            p = jnp.exp(s - m_new)
            alpha = jnp.exp(m_old - m_new)
            l_scr[h:h + 1, :] = l_scr[h:h + 1, :] * alpha + jnp.sum(p, axis=0, keepdims=True)
            pv = jnp.dot(vT_ref[r, pl.ds(k0, ck)], p.astype(BF16), preferred_element_type=F32)
            outT_scr[r, :] = outT_scr[r, :] * alpha + pv
            m_scr[h:h + 1, :] = m_new
        return 0

    lax.fori_loop(0, j + 1, chunk_body, 0)
    out = outT_scr[...].reshape(N_HEADS, HEAD_DIM, tq) / l_scr[...][:, None, :]
    o_ref[...] = out.reshape(N_HEADS * HEAD_DIM, tq).T.astype(o_ref.dtype)


def _prompt_attention(qT, kT, vT, qiT, ki, wT, topk):
    B, _, L = qT.shape
    tq = ATTN_BLOCK
    nq = L // tq
    feat = lambda b, j: (b, 0, j)
    whole = lambda b, j: (b, 0, 0)
    return pl.pallas_call(
        functools.partial(_prompt_attn_kernel, topk=topk),
        name="prompt_attn",
        grid=(B, nq),
        in_specs=[pl.BlockSpec((None, D_ATTN, tq), feat),
                  pl.BlockSpec((None, D_ATTN, L), whole),
                  pl.BlockSpec((None, D_ATTN, L), whole),
                  pl.BlockSpec((None, N_IDX_HEADS * IDX_DIM, tq), feat),
                  pl.BlockSpec((None, L, IDX_DIM), whole),
                  pl.BlockSpec((None, N_IDX_HEADS, tq), feat)],
        out_specs=pl.BlockSpec((tq, D_ATTN), lambda b, j: (b * nq + j, 0)),
        out_shape=jax.ShapeDtypeStruct((B * L, D_ATTN), BF16),
        scratch_shapes=[pltpu.VMEM((L, tq), I32), pltpu.VMEM((L, tq), F32),
                        pltpu.VMEM((D_ATTN, tq), F32),
                        pltpu.VMEM((N_HEADS, tq), F32), pltpu.VMEM((N_HEADS, tq), F32),
                        pltpu.VMEM((2, tq, tq), F32)],
        compiler_params=_cparams(("parallel", "arbitrary")),
    )(qT, kT, vT, qiT, ki, wT)


IDX_PAGES_PER_STEP = 16
ATT_PAGES_PER_STEP = 8
HEADS_PER_GROUP = 4


def _sample_index_kernel(pt_ref, q_ref, w_ref, kin_ref, *rest, n_pages_step, topk):
    del pt_ref
    page_refs = rest[:n_pages_step]
    bias_ref, key_scr = rest[n_pages_step], rest[n_pages_step + 1]
    g = pl.program_id(1)
    ng = pl.num_programs(1)
    n_tok = bias_ref.shape[0]
    q = q_ref[...]
    w = w_ref[...]

    def scores(kT):
        d = jnp.dot(q, kT.astype(BF16), preferred_element_type=F32)
        r = jnp.maximum(d, 0.0) * w
        return jnp.sum(r.reshape(N_IDX_HEADS, n_tok, r.shape[1]), axis=0)

    for c in range(n_pages_step):
        col = pl.multiple_of((g * n_pages_step + c) * PAGE_SIZE, PAGE_SIZE)
        key_scr[:, pl.ds(col, PAGE_SIZE)] = _sortable_key(scores(page_refs[c][...]))

    @pl.when(g == ng - 1)
    def _():
        n_past = key_scr.shape[1] - PAGE_SIZE
        t = lax.broadcasted_iota(I32, (n_tok, PAGE_SIZE), 0)
        jn = lax.broadcasted_iota(I32, (n_tok, PAGE_SIZE), 1)
        key_scr[:, n_past:] = jnp.where(jn <= t, _sortable_key(scores(kin_ref[...])), INT_MIN)

        def count_ge(cand):
            return jnp.sum((key_scr[...] >= cand).astype(I32), axis=1, keepdims=True)

        thr = jnp.maximum(_kth_largest_key(count_ge, topk, (n_tok, 1)), INT_MIN + 1)
        bias_ref[...] = jnp.where(key_scr[...] >= thr, 0.0, NEG_BIG)


def _sample_select(page_table, q_rows, w_rows, kin_T, cache_kidx_T, topk):
    Bs, n_pages = page_table.shape
    n = IDX_PAGES_PER_STEP
    T = q_rows.shape[1] // N_IDX_HEADS
    width = (n_pages + 1) * PAGE_SIZE
    per_seq = lambda b, g, pt: (b, 0, 0)
    page_specs = [pl.BlockSpec((None, IDX_DIM, PAGE_SIZE), lambda b, g, pt, c=c: (pt[b, g * n + c], 0, 0))
                  for c in range(n)]
    return pl.pallas_call(
        functools.partial(_sample_index_kernel, n_pages_step=n, topk=topk),
        name="sample_select",
        grid_spec=pltpu.PrefetchScalarGridSpec(
            num_scalar_prefetch=1,
            grid=(Bs, n_pages // n),
            in_specs=[pl.BlockSpec((None,) + q_rows.shape[1:], per_seq),
                      pl.BlockSpec((None,) + w_rows.shape[1:], per_seq),
                      pl.BlockSpec((None, IDX_DIM, PAGE_SIZE), per_seq)] + page_specs,
            out_specs=pl.BlockSpec((None, T, width), per_seq),
            scratch_shapes=[pltpu.VMEM((T, width), I32)]),
        out_shape=jax.ShapeDtypeStruct((Bs, T, width), F32),
        compiler_params=_cparams(("parallel", "arbitrary")),
    )(page_table, q_rows, w_rows, kin_T, *([cache_kidx_T] * n))


def _sample_attn_kernel(pt_ref, qbd_ref, bias_ref, biasn_ref, kn_ref, vn_ref, *rest, n_pages_step):
    del pt_ref
    k_refs, v_refs = rest[:n_pages_step], rest[n_pages_step:2 * n_pages_step]
    o_ref, m_scr, l_scr, acc_scr = rest[2 * n_pages_step:]
    g = pl.program_id(1)
    ng = pl.num_programs(1)
    n_tok = bias_ref.shape[0]
    hg = HEADS_PER_GROUP
    rows_g = hg * n_tok

    @pl.when(g == 0)
    def _():
        m_scr[...] = jnp.full(m_scr.shape, NEG_BIG, F32)
        l_scr[...] = jnp.zeros(l_scr.shape, F32)
        acc_scr[...] = jnp.zeros(acc_scr.shape, F32)

    def process(pages):
        for grp in range(N_HEADS // hg):
            qg = qbd_ref[grp]
            rows = slice(grp * rows_g, (grp + 1) * rows_g)
            s_list = []
            for k_ref, _, b in pages:
                kt = k_ref[grp * hg:(grp + 1) * hg].reshape(hg * HEAD_DIM, PAGE_SIZE).astype(BF16)
                s = jnp.dot(qg, kt, preferred_element_type=F32)
                s_list.append((s.reshape(hg, n_tok, PAGE_SIZE) + b[None]).reshape(rows_g, PAGE_SIZE))
            m_old = m_scr[rows, :]
            m_new = jnp.maximum(m_old, jnp.max(functools.reduce(jnp.maximum, s_list), axis=1, keepdims=True))
            alpha = jnp.exp(m_old - m_new)
            psum = jnp.zeros((rows_g, PAGE_SIZE), F32)
            pv = jnp.zeros((rows_g, hg * HEAD_DIM), F32)
            for s, (_, v_ref, _) in zip(s_list, pages):
                p = jnp.exp(s - m_new)
                psum = psum + p
                vt = v_ref[grp * hg:(grp + 1) * hg].reshape(hg * HEAD_DIM, PAGE_SIZE).astype(BF16)
                pv = pv + lax.dot_general(p.astype(BF16), vt, (((1,), (1,)), ((), ())),
                                          preferred_element_type=F32)
            l_scr[rows, :] = l_scr[rows, :] * alpha + jnp.sum(psum, axis=1, keepdims=True)
            acc_scr[rows, :] = acc_scr[rows, :] * alpha + pv
            m_scr[rows, :] = m_new

    process([(k_refs[c], v_refs[c], bias_ref[:, c * PAGE_SIZE:(c + 1) * PAGE_SIZE])
             for c in range(n_pages_step)])

    @pl.when(g == ng - 1)
    def _():
        process([(kn_ref, vn_ref, biasn_ref[...])])
        o_ref[...] = acc_scr[...] / l_scr[...]


def _sample_attention(page_table, qbd, bias, kT_new, vT_new, cache_kT, cache_vT):
    Bs, n_pages = page_table.shape
    n = ATT_PAGES_PER_STEP
    T = bias.shape[1]
    rows = N_HEADS * T
    per_seq4 = lambda b, g, pt: (b, 0, 0, 0)
    page_block = (None, N_HEADS, HEAD_DIM, PAGE_SIZE)
    page_specs = [pl.BlockSpec(page_block, lambda b, g, pt, c=c: (pt[b, g * n + c], 0, 0, 0)) for c in range(n)]
    return pl.pallas_call(
        functools.partial(_sample_attn_kernel, n_pages_step=n),
        name="sample_attn",
        grid_spec=pltpu.PrefetchScalarGridSpec(
            num_scalar_prefetch=1,
            grid=(Bs, n_pages // n),
            in_specs=[pl.BlockSpec((None,) + qbd.shape[1:], per_seq4),
                      pl.BlockSpec((None, T, n * PAGE_SIZE), lambda b, g, pt: (b, 0, g)),
                      pl.BlockSpec((None, T, PAGE_SIZE), lambda b, g, pt: (b, 0, n_pages)),
                      pl.BlockSpec(page_block, per_seq4),
                      pl.BlockSpec(page_block, per_seq4)] + page_specs + page_specs,
            out_specs=pl.BlockSpec((None, rows, HEADS_PER_GROUP * HEAD_DIM), lambda b, g, pt: (b, 0, 0)),
            scratch_shapes=[pltpu.VMEM((rows, 1), F32), pltpu.VMEM((rows, 1), F32),
                            pltpu.VMEM((rows, HEADS_PER_GROUP * HEAD_DIM), F32)]),
        out_shape=jax.ShapeDtypeStruct((Bs, rows, HEADS_PER_GROUP * HEAD_DIM), F32),
        compiler_params=_cparams(("parallel", "arbitrary")),
    )(page_table, qbd, bias, bias, kT_new, vT_new, *([cache_kT] * n), *([cache_vT] * n))


def _silu(x):
    return x * jax.nn.sigmoid(x)


def _rms(x):
    return lax.rsqrt(jnp.mean(x * x, axis=-1, keepdims=True) + EPS)


def _adaln(c_all, w_ada, b_ada):
    n = c_all.shape[0]
    n_pad = -(-n // 8) * 8
    c_pad = jnp.pad(c_all, ((0, n_pad - n), (0, 0)))
    tn = 1536
    (mod,) = _mm(c_pad, w_ada, trans_b=False, tm=n_pad, tn=tn, prologue=_silu,
                 extras=[(b_ada.reshape(1, -1), pl.BlockSpec((1, tn), lambda i, j: (0, j)))],
                 outs=[(jax.ShapeDtypeStruct((n_pad, w_ada.shape[1]), F32), _spec_mn(n_pad, tn))],
                 epilogue=lambda acc, b: (acc + b,))
    return mod[:n]


def _ki_wi_epilogue(acc, g):
    ki = acc[:, :IDX_DIM]
    wi = acc[:, IDX_DIM:IDX_DIM + N_IDX_HEADS] * (N_IDX_HEADS ** -0.5 * IDX_DIM ** -0.5)
    return ki * _rms(ki) * g, wi


def _small_proj(h, wT_small, idx_k_g, tm):
    M = h.shape[0]
    n = IDX_DIM + N_IDX_HEADS
    return _mm(h, wT_small, trans_b=True, tm=tm, tn=n,
               extras=[(idx_k_g.reshape(1, IDX_DIM), pl.BlockSpec((1, IDX_DIM), lambda i, j: (0, 0)))],
               outs=[(jax.ShapeDtypeStruct((M, IDX_DIM), F32), pl.BlockSpec((tm, IDX_DIM), lambda i, j: (i, 0))),
                     (jax.ShapeDtypeStruct((M, N_IDX_HEADS), F32),
                      pl.BlockSpec((tm, N_IDX_HEADS), lambda i, j: (i, 0)))],
               epilogue=_ki_wi_epilogue)


def _proj_T(wT, row_off, n_rows, h, n_seq, L, scale, want_f32):
    tr, tc = 512, 1024
    cps = L // tc
    spec = pl.BlockSpec((None, tr, tc), lambda i, j: (j // cps, i, j % cps))
    outs = [(jax.ShapeDtypeStruct((n_seq, n_rows, L), BF16), spec)]
    if want_f32:
        outs.append((jax.ShapeDtypeStruct((n_seq, n_rows, L), F32), spec))
    ep = (lambda acc: (acc * scale,) * len(outs)) if scale != 1.0 else (lambda acc: (acc,) * len(outs))
    return _mm(wT, h, trans_b=True, tm=tr, tn=tc, a_row_off=row_off // tr, m=n_rows, outs=outs, epilogue=ep)


def _tail(x_flat, m, w_o, w_ff1, w_ff2, norm2_g, final_g, g1, sc2, sh2, g2, tm_o, tm_f, rows_per_seq):
    M, D = x_flat.shape
    gvec = pl.BlockSpec((1, D), lambda i, j: (0, 0))
    tn_o, tn_f = 512, 256
    rv = _RowVec(tm_o, rows_per_seq)
    x1, h2 = _mm_rows(
        m, w_o, tm=tm_o, tn=tn_o,
        step_extras=[(x_flat, _spec_mn(tm_o, tn_o)), (g1, rv.cols(tn_o))],
        fin_extras=[(norm2_g.reshape(1, D), gvec), (sc2, rv.full(D)), (sh2, rv.full(D))],
        out_dtypes=[F32, BF16],
        step_fn=lambda d, x, g1_: x + g1_ * d,
        final_fn=lambda x1_, n2g, sc2_, sh2_: (None, (x1_ * _rms(x1_) * n2g) * (1.0 + sc2_) + sh2_))

    tm1 = min(1024, M)
    (a,) = _mm(h2, w_ff1, trans_b=False, tm=tm1, tn=512,
               outs=[(jax.ShapeDtypeStruct((M, D_FF), BF16), _spec_mn(tm1, 512))],
               epilogue=lambda acc: (jnp.square(jnp.maximum(acc, 0.0)),))

    rv = _RowVec(tm_f, rows_per_seq)
    (y,) = _mm_rows(
        a, w_ff2, tm=tm_f, tn=tn_f,
        step_extras=[(x1, _spec_mn(tm_f, tn_f)), (g2, rv.cols(tn_f))],
        fin_extras=[(final_g.reshape(1, D), gvec)],
        out_dtypes=[F32],
        step_fn=lambda d, x1_, g2_: x1_ + g2_ * d,
        final_fn=lambda x2, fg: (x2 * _rms(x2) * fg,))
    return y


def kernel(x_prompt, x_sample, cache_k, cache_v, cache_k_idx, state_conv, page_table,
           c_prompt, c_sample, w_ada, b_ada, norm1_g, norm2_g, w_in, conv_w, idx_k_g,
           w_bc, w_ba, w_o, w_ff1, w_ff2, final_g):
    B, L, D = x_prompt.shape
    Bs, T, _ = x_sample.shape
    Mp, Ms = B * L, Bs * T
    l = 0
    wT = jnp.transpose(w_in[l])
    wT_gates = wT[OFF_GCONV:]
    wT_small = wT[OFF_KI:OFF_GCONV]

    mod = _adaln(jnp.concatenate([c_prompt, c_sample], axis=0), w_ada[l], b_ada[l])
    mod_p = [v[:, None, :] for v in jnp.split(mod[:B], 6, axis=-1)]
    mod_s = [v[:, None, :] for v in jnp.split(mod[B:], 6, axis=-1)]

    sh1, sc1, g1, sh2, sc2, g2 = mod_p
    tm = 1024
    h = _norm_mod(x_prompt, norm1_g[l], sc1, sh1, tm)
    (zA,) = _mm(h, wT, trans_b=True, tm=tm, tn=512, n=3 * D_CONV,
                outs=[(jax.ShapeDtypeStruct((Mp, 3 * D_CONV), F32), _spec_mn(tm, 512))],
                epilogue=lambda acc: (acc,))
    (zG,) = _mm(h, wT_gates, trans_b=True, tm=tm, tn=512,
                outs=[(jax.ShapeDtypeStruct((Mp, 2 * D), F32), _spec_mn(tm, 512))],
                epilogue=lambda acc: (acc,))
    ki_p, wi_p = _small_proj(h, wT_small, idx_k_g[l], tm)
    (qT,) = _proj_T(wT, OFF_Q, D_ATTN, h, B, L, HEAD_DIM ** -0.5, False)
    kT, kT32 = _proj_T(wT, OFF_K, D_ATTN, h, B, L, 1.0, True)
    vT, vT32 = _proj_T(wT, OFF_V, D_ATTN, h, B, L, 1.0, True)
    (qiT,) = _proj_T(wT, OFF_QI, N_IDX_HEADS * IDX_DIM, h, B, L, 1.0, False)

    conv0 = jnp.zeros((B, CONV_W - 1, D_CONV), F32)
    yc, cv_p = _gated_conv(zA, conv0, conv_w[l], B, L, tm)
    wT_p = jnp.transpose(wi_p.reshape(B, L, N_IDX_HEADS), (0, 2, 1))
    ya = _prompt_attention(qT, kT, vT, qiT, ki_p.reshape(B, L, IDX_DIM), wT_p, min(TOPK_MAX, L // 4))
    m = _merge(yc, ya, w_bc[l], w_ba[l], zG, tm, 512)
    y_prompt = _tail(x_prompt.reshape(Mp, D), m, w_o[l], w_ff1[l], w_ff2[l], norm2_g[l], final_g,
                     g1, sc2, sh2, g2, 1024, 512, L).reshape(B, L, D)
    k_prompt = jnp.transpose(kT32.reshape(B, N_HEADS, HEAD_DIM, L), (0, 3, 1, 2))
    v_prompt = jnp.transpose(vT32.reshape(B, N_HEADS, HEAD_DIM, L), (0, 3, 1, 2))
    kidx_prompt = ki_p.reshape(B, L, IDX_DIM)

    sh1, sc1, g1, sh2, sc2, g2 = mod_s
    hs = _norm_mod(x_sample, norm1_g[l], sc1, sh1, T)
    (zs,) = _mm(hs, wT, trans_b=True, tm=Ms, tn=512,
                outs=[(jax.ShapeDtypeStruct((Ms, N_IN), F32), _spec_mn(Ms, 512))],
                epilogue=lambda acc: (acc,))
    ki_s, wi_s = _small_proj(hs, wT_small, idx_k_g[l], Ms)
    ycs, cv_s = _gated_conv(zs, state_conv[l], conv_w[l], Bs, T, T)
    q_s = zs[:, OFF_Q:OFF_K].reshape(Bs, T, N_HEADS, HEAD_DIM)
    k_s = zs[:, OFF_K:OFF_V].reshape(Bs, T, N_HEADS, HEAD_DIM)
    v_s = zs[:, OFF_V:OFF_QI].reshape(Bs, T, N_HEADS, HEAD_DIM)
    qi_s = zs[:, OFF_QI:OFF_KI].reshape(Bs, T, N_IDX_HEADS, IDX_DIM)
    ki_s3 = ki_s.reshape(Bs, T, IDX_DIM)
    lane_pad = lambda a_: jnp.pad(a_, [(0, 0)] * (a_.ndim - 1) + [(0, PAGE_SIZE - T)])
    qi_rows = jnp.transpose(qi_s, (0, 2, 1, 3)).reshape(Bs, N_IDX_HEADS * T, IDX_DIM).astype(BF16)
    wi_rows = jnp.transpose(wi_s.reshape(Bs, T, N_IDX_HEADS), (0, 2, 1)).reshape(Bs, N_IDX_HEADS * T, 1)
    kin_T = lane_pad(jnp.transpose(ki_s3, (0, 2, 1)))
    n_grp = N_HEADS // HEADS_PER_GROUP
    qg = jnp.transpose((q_s * HEAD_DIM ** -0.5).reshape(Bs, T, n_grp, HEADS_PER_GROUP, HEAD_DIM), (0, 2, 3, 1, 4))
    qbd = jnp.einsum('bghtd,hk->bghtkd', qg, jnp.eye(HEADS_PER_GROUP, dtype=F32))
    qbd = qbd.reshape(Bs, n_grp, HEADS_PER_GROUP * T, HEADS_PER_GROUP * HEAD_DIM).astype(BF16)
    kT_new = lane_pad(jnp.transpose(k_s, (0, 2, 3, 1)))
    vT_new = lane_pad(jnp.transpose(v_s, (0, 2, 3, 1)))
    n_pages = page_table.shape[1]
    topk_s = min(TOPK_MAX, (n_pages * PAGE_SIZE + T) // 4)
    bias = _sample_select(page_table, qi_rows, wi_rows, kin_T,
                          jnp.transpose(cache_k_idx[l], (0, 2, 1)), topk_s)
    att = _sample_attention(page_table, qbd, bias, kT_new, vT_new,
                            jnp.transpose(cache_k[l], (0, 2, 3, 1)), jnp.transpose(cache_v[l], (0, 2, 3, 1)))
    att = att.reshape(Bs, n_grp, HEADS_PER_GROUP, T, HEADS_PER_GROUP, HEAD_DIM)
    att = jnp.diagonal(att, axis1=2, axis2=4)
    ya_s = jnp.transpose(att, (0, 2, 1, 4, 3)).reshape(Ms, D_ATTN).astype(BF16)
    m_s = _merge(ycs, ya_s, w_bc[l], w_ba[l], zs[:, OFF_GCONV:], Ms, 512)
    tok = lambda v: jnp.broadcast_to(v, (Bs, T, D)).reshape(Ms, D)
    y_sample = _tail(x_sample.reshape(Ms, D), m_s, w_o[l], w_ff1[l], w_ff2[l], norm2_g[l], final_g,
                     tok(g1), tok(sc2), tok(sh2), tok(g2), Ms, Ms, None).reshape(Bs, T, D)

    st = lambda a_: a_[None]
    return (y_prompt, y_sample, st(k_prompt), st(v_prompt), st(kidx_prompt), st(cv_p),
            st(k_s), st(v_s), st(ki_s3), st(cv_s))
```

```python
import functools

import jax
import jax.numpy as jnp
import numpy as np
from jax import lax
from jax.experimental import pallas as pl
from jax.experimental.pallas import tpu as pltpu

D_MODEL = 2048
PAGE_SIZE = 128
D_CONV = 1024
CONV_W = 3
N_HEADS = 16
HEAD_DIM = 64
D_ATTN = N_HEADS * HEAD_DIM
N_IDX_HEADS = 16
IDX_DIM = 64
TOPK_MAX = 256
D_FF = 4 * D_MODEL
EPS = 1e-6
IN_SIZES = (D_CONV, D_CONV, D_CONV, D_ATTN, D_ATTN, D_ATTN,
            N_IDX_HEADS * IDX_DIM, IDX_DIM, N_IDX_HEADS, D_MODEL, D_MODEL)
IN_OFFS = tuple(int(v) for v in np.cumsum((0,) + IN_SIZES))
OFF_U, OFF_GB, OFF_GC, OFF_Q, OFF_K, OFF_V, OFF_QI, OFF_KI, OFF_WI, OFF_GCONV, OFF_GATTN, N_IN = IN_OFFS

BF16 = jnp.bfloat16
F32 = jnp.float32
I32 = jnp.int32
INT_MIN = -2 ** 31
NEG_BIG = -1e30
VMEM_LIMIT_BYTES = 56 * 1024 * 1024
ATTN_BLOCK = 256


def _cparams(sem):
    return pltpu.CompilerParams(dimension_semantics=sem, vmem_limit_bytes=VMEM_LIMIT_BYTES)


def _mm_kernel(a_ref, b_ref, *refs, n_extra, trans_b, prologue, epilogue):
    extra, outs = refs[:n_extra], refs[n_extra:]
    a = a_ref[...]
    if prologue is not None:
        a = prologue(a)
    dims = (((1,), (1,)), ((), ())) if trans_b else (((1,), (0,)), ((), ()))
    d = lax.dot_general(a.astype(BF16), b_ref[...].astype(BF16), dims, preferred_element_type=F32)
    for o, r in zip(outs, epilogue(d, *[e[...] for e in extra])):
        o[...] = r.astype(o.dtype)


def _mm(a, b, *, trans_b, tm, tn, a_row_off=0, extras=(), outs, epilogue, prologue=None, m=None, n=None):
    kdim = a.shape[1]
    m = a.shape[0] if m is None else m
    if n is None:
        n = b.shape[0] if trans_b else b.shape[1]
    b_spec = pl.BlockSpec((tn, kdim), lambda i, j: (j, 0)) if trans_b else pl.BlockSpec((kdim, tn), lambda i, j: (0, j))
    return pl.pallas_call(
        functools.partial(_mm_kernel, n_extra=len(extras), trans_b=trans_b, prologue=prologue, epilogue=epilogue),
        grid=(pl.cdiv(m, tm), pl.cdiv(n, tn)),
        in_specs=[pl.BlockSpec((tm, kdim), lambda i, j: (i + a_row_off, 0)), b_spec] + [s for _, s in extras],
        out_specs=[s for _, s in outs],
        out_shape=[o for o, _ in outs],
        compiler_params=_cparams(("parallel", "arbitrary")),
    )(a, b, *[e for e, _ in extras])


def _mm_rows_kernel(a_ref, b_ref, *refs, n_step, n_fin, step_fn, final_fn):
    steps, fins, outs = refs[:n_step], refs[n_step:n_step + n_fin], refs[n_step + n_fin:]
    j = pl.program_id(1)
    tn = b_ref.shape[1]
    d = jnp.dot(a_ref[...], b_ref[...].astype(BF16), preferred_element_type=F32)
    outs[0][:, pl.ds(pl.multiple_of(j * tn, tn), tn)] = step_fn(d, *[e[...] for e in steps])

    @pl.when(j == pl.num_programs(1) - 1)
    def _():
        for o, r in zip(outs, final_fn(outs[0][...], *[f[...] for f in fins])):
            if r is not None:
                o[...] = r.astype(o.dtype)


def _mm_rows(a, b, *, tm, tn, step_extras, fin_extras, out_dtypes, step_fn, final_fn):
    M, kdim = a.shape
    N = b.shape[1]
    row = pl.BlockSpec((tm, N), lambda i, j: (i, 0))
    return pl.pallas_call(
        functools.partial(_mm_rows_kernel, n_step=len(step_extras), n_fin=len(fin_extras),
                          step_fn=step_fn, final_fn=final_fn),
        grid=(M // tm, N // tn),
        in_specs=[pl.BlockSpec((tm, kdim), lambda i, j: (i, 0)), pl.BlockSpec((kdim, tn), lambda i, j: (0, j))]
        + [s for _, s in step_extras] + [s for _, s in fin_extras],
        out_specs=[row] * len(out_dtypes),
        out_shape=[jax.ShapeDtypeStruct((M, N), dt) for dt in out_dtypes],
        compiler_params=_cparams(("parallel", "arbitrary")),
    )(a, b, *[e for e, _ in step_extras], *[e for e, _ in fin_extras])


def _spec_mn(tm, tn):
    return pl.BlockSpec((tm, tn), lambda i, j: (i, j))


class _RowVec:
    def __init__(self, tm, rows_per_seq):
        self.tm, self.bps = tm, (None if rows_per_seq is None else rows_per_seq // tm)

    def cols(self, tn):
        if self.bps is None:
            return pl.BlockSpec((self.tm, tn), lambda i, j: (i, j))
        return pl.BlockSpec((None, 1, tn), lambda i, j: (i // self.bps, 0, j))

    def full(self, n):
        if self.bps is None:
            return pl.BlockSpec((self.tm, n), lambda i, j: (i, 0))
        return pl.BlockSpec((None, 1, n), lambda i, j: (i // self.bps, 0, 0))


def _norm_mod_kernel(x_ref, g_ref, sc_ref, sh_ref, o_ref):
    x = x_ref[...]
    r = lax.rsqrt(jnp.mean(x * x, axis=-1, keepdims=True) + EPS)
    o_ref[...] = ((x * r * g_ref[...]) * (1.0 + sc_ref[...]) + sh_ref[...]).astype(o_ref.dtype)


def _norm_mod(x, g, sc, sh, tm):
    n_seq, L, D = x.shape
    bps = L // tm
    return pl.pallas_call(
        _norm_mod_kernel,
        grid=(n_seq, bps),
        in_specs=[pl.BlockSpec((None, tm, D), lambda s, i: (s, i, 0)),
                  pl.BlockSpec((1, D), lambda s, i: (0, 0)),
                  pl.BlockSpec((None, 1, D), lambda s, i: (s, 0, 0)),
                  pl.BlockSpec((None, 1, D), lambda s, i: (s, 0, 0))],
        out_specs=pl.BlockSpec((tm, D), lambda s, i: (s * bps + i, 0)),
        out_shape=jax.ShapeDtypeStruct((n_seq * L, D), BF16),
        compiler_params=_cparams(("parallel", "arbitrary")),
    )(x, g.reshape(1, D), sc, sh)


def _conv_kernel(u_ref, gb_ref, gc_ref, prev_ref, w_ref, y_ref, new_ref, carry_ref):
    i = pl.program_id(1)
    tm = u_ref.shape[0]

    @pl.when(i == 0)
    def _():
        carry_ref[...] = prev_ref[...]

    cu = gc_ref[...] * u_ref[...]
    row = lax.broadcasted_iota(I32, cu.shape, 0)
    c0 = carry_ref[0:1, :]
    c1 = carry_ref[1:2, :]
    s1 = jnp.where(row == 0, c1, pltpu.roll(cu, 1, 0))
    s2 = jnp.where(row == 0, c0, jnp.where(row == 1, c1, pltpu.roll(cu, 2, 0)))
    conv = w_ref[0:1, :] * s2 + w_ref[1:2, :] * s1 + w_ref[2:3, :] * cu
    y_ref[...] = (gb_ref[...] * conv).astype(y_ref.dtype)
    last2 = cu[tm - 2:tm, :]
    carry_ref[...] = last2
    new_ref[...] = last2


def _gated_conv(z, prev, conv_w, n_seq, L, tm):
    C = D_CONV
    bps = L // tm
    row = lambda s, i: s * bps + i
    return pl.pallas_call(
        _conv_kernel,
        grid=(n_seq, bps),
        in_specs=[pl.BlockSpec((tm, C), lambda s, i: (row(s, i), 0)),
                  pl.BlockSpec((tm, C), lambda s, i: (row(s, i), 1)),
                  pl.BlockSpec((tm, C), lambda s, i: (row(s, i), 2)),
                  pl.BlockSpec((None, 2, C), lambda s, i: (s, 0, 0)),
                  pl.BlockSpec((CONV_W, C), lambda s, i: (0, 0))],
        out_specs=[pl.BlockSpec((tm, C), lambda s, i: (row(s, i), 0)),
                   pl.BlockSpec((None, 2, C), lambda s, i: (s, 0, 0))],
        out_shape=[jax.ShapeDtypeStruct((n_seq * L, C), BF16),
                   jax.ShapeDtypeStruct((n_seq, 2, C), F32)],
        scratch_shapes=[pltpu.VMEM((2, C), F32)],
        compiler_params=_cparams(("arbitrary", "arbitrary")),
    )(z, z, z, prev, conv_w)


def _merge_kernel(yc_ref, ya_ref, wbc_ref, wba_ref, gc_ref, ga_ref, o_ref):
    dc = jnp.dot(yc_ref[...], wbc_ref[...].astype(BF16), preferred_element_type=F32)
    da = jnp.dot(ya_ref[...], wba_ref[...].astype(BF16), preferred_element_type=F32)
    o_ref[...] = (jax.nn.sigmoid(gc_ref[...]) * dc + jax.nn.sigmoid(ga_ref[...]) * da).astype(o_ref.dtype)


def _merge(yc, ya, w_bc, w_ba, zg, tm, tn):
    M = yc.shape[0]
    nb = D_MODEL // tn
    return pl.pallas_call(
        _merge_kernel,
        grid=(M // tm, nb),
        in_specs=[pl.BlockSpec((tm, D_CONV), lambda i, j: (i, 0)),
                  pl.BlockSpec((tm, D_ATTN), lambda i, j: (i, 0)),
                  pl.BlockSpec((D_CONV, tn), lambda i, j: (0, j)),
                  pl.BlockSpec((D_ATTN, tn), lambda i, j: (0, j)),
                  pl.BlockSpec((tm, tn), lambda i, j: (i, j)),
                  pl.BlockSpec((tm, tn), lambda i, j: (i, j + nb))],
        out_specs=pl.BlockSpec((tm, tn), lambda i, j: (i, j)),
        out_shape=jax.ShapeDtypeStruct((M, D_MODEL), BF16),
        compiler_params=_cparams(("parallel", "arbitrary")),
    )(yc, ya, w_bc, w_ba, zg, zg)


def _sortable_key(x):
    bits = lax.bitcast_convert_type(x, I32)
    return bits ^ ((bits >> 31) & 0x7FFFFFFF)


def _kth_largest_key(count_ge, k, shape):
    base = jnp.where(count_ge(jnp.zeros(shape, I32)) >= k, 0, INT_MIN).astype(I32)

    def bit_body(t, base):
        cand = base + jnp.left_shift(jnp.int32(1), 30 - t)
        return jnp.where(count_ge(cand) >= k, cand, base)

    return lax.fori_loop(0, 31, bit_body, base)


def _prompt_attn_kernel(qT_ref, kT_ref, vT_ref, qiT_ref, ki_ref, wT_ref, o_ref,
                        key_scr, bias_scr, outT_scr, m_scr, l_scr, s_scr, *, topk):
    j = pl.program_id(1)
    tq = qT_ref.shape[1]
    ck = tq
    sub = ck // 2

    def score_body(c, _):
        k0 = pl.multiple_of(c * sub, sub)
        ki_c = ki_ref[pl.ds(k0, sub), :].astype(BF16)
        acc = jnp.zeros((sub, tq), F32)
        for h in range(N_IDX_HEADS):
            d = jnp.dot(ki_c, qiT_ref[h * IDX_DIM:(h + 1) * IDX_DIM, :], preferred_element_type=F32)
            acc = acc + jnp.maximum(d, 0.0) * wT_ref[h:h + 1, :]
        kpos = k0 + lax.broadcasted_iota(I32, (sub, tq), 0)
        qpos = j * tq + lax.broadcasted_iota(I32, (sub, tq), 1)
        key_scr[pl.ds(k0, sub), :] = jnp.where(kpos <= qpos, _sortable_key(acc), INT_MIN)
        return 0

    lax.fori_loop(0, 2 * (j + 1), score_body, 0)

    def count_ge(cand):
        def body(c, cnt):
            blk = key_scr[pl.ds(pl.multiple_of(c * ck, ck), ck), :]
            return cnt + jnp.sum((blk >= cand).astype(I32), axis=0, keepdims=True)
        return lax.fori_loop(0, j + 1, body, jnp.zeros((1, tq), I32))

    thr = _kth_largest_key(count_ge, topk, (1, tq))
    thr = jnp.maximum(thr, INT_MIN + 1)
    n_ge = count_ge(thr)
    n_gt = count_ge(thr + 1)
    need = topk - n_gt

    def bias_body(c, _):
        k0 = pl.multiple_of(c * ck, ck)
        bias_scr[pl.ds(k0, ck), :] = jnp.where(key_scr[pl.ds(k0, ck), :] >= thr, 0.0, NEG_BIG)
        return 0

    lax.fori_loop(0, j + 1, bias_body, 0)

    @pl.when(jnp.max(n_ge - n_gt - need) > 0)
    def _():
        tri = (lax.broadcasted_iota(I32, (ck, ck), 1) < lax.broadcasted_iota(I32, (ck, ck), 0)).astype(BF16)
        need_f = need.astype(F32)

        def tie_body(c, seen):
            k0 = pl.multiple_of(c * ck, ck)
            blk = key_scr[pl.ds(k0, ck), :]
            eq = blk == thr
            eq_f = jnp.where(eq, 1.0, 0.0)
            before = seen + jnp.dot(tri, eq_f.astype(BF16), preferred_element_type=F32)
            keep = (blk > thr) | (eq & (before < need_f))
            bias_scr[pl.ds(k0, ck), :] = jnp.where(keep, 0.0, NEG_BIG)
            return seen + jnp.sum(eq_f, axis=0, keepdims=True)

        lax.fori_loop(0, j + 1, tie_body, jnp.zeros((1, tq), F32))

    m_scr[...] = jnp.full(m_scr.shape, NEG_BIG, F32)
    l_scr[...] = jnp.zeros(l_scr.shape, F32)
    outT_scr[...] = jnp.zeros(outT_scr.shape, F32)

    def chunk_body(c, _):
        k0 = pl.multiple_of(c * ck, ck)

        def logits(h):
            r = slice(h * HEAD_DIM, (h + 1) * HEAD_DIM)
            return lax.dot_general(kT_ref[r, pl.ds(k0, ck)], qT_ref[r, :], (((0,), (0,)), ((), ())),
                                   preferred_element_type=F32)

        s_scr[0] = logits(0)
        for h in range(N_HEADS):
            r = slice(h * HEAD_DIM, (h + 1) * HEAD_DIM)
            if h + 1 < N_HEADS:
                s_scr[(h + 1) % 2] = logits(h + 1)
            s = s_scr[h % 2] + bias_scr[pl.ds(k0, ck), :]
            m_old = m_scr[h:h + 1, :]
            m_new = jnp.maximum(m_old, jnp.max(s, axis=0, keepdims=True))
            p = jnp.exp(s - m_new)
            alpha = jnp.exp(m_old - m_new)
            l_scr[h:h + 1, :] = l_scr[h:h + 1, :] * alpha + jnp.sum(p, axis=0, keepdims=True)
            pv = jnp.dot(vT_ref[r, pl.ds(k0, ck)], p.astype(BF16), preferred_element_type=F32)
            outT_scr[r, :] = outT_scr[r, :] * alpha + pv
            m_scr[h:h + 1, :] = m_new
        return 0

    lax.fori_loop(0, j + 1, chunk_body, 0)
    out = outT_scr[...].reshape(N_HEADS, HEAD_DIM, tq) / l_scr[...][:, None, :]
    o_ref[...] = out.reshape(N_HEADS * HEAD_DIM, tq).T.astype(o_ref.dtype)


def _prompt_attention(qT, kT, vT, qiT, ki, wT, topk):
    B, _, L = qT.shape
    tq = ATTN_BLOCK
    nq = L // tq
    feat = lambda b, j: (b, 0, j)
    whole = lambda b, j: (b, 0, 0)
    return pl.pallas_call(
        functools.partial(_prompt_attn_kernel, topk=topk),
        name="prompt_attn",
        grid=(B, nq),
        in_specs=[pl.BlockSpec((None, D_ATTN, tq), feat),
                  pl.BlockSpec((None, D_ATTN, L), whole),
                  pl.BlockSpec((None, D_ATTN, L), whole),
                  pl.BlockSpec((None, N_IDX_HEADS * IDX_DIM, tq), feat),
                  pl.BlockSpec((None, L, IDX_DIM), whole),
                  pl.BlockSpec((None, N_IDX_HEADS, tq), feat)],
        out_specs=pl.BlockSpec((tq, D_ATTN), lambda b, j: (b * nq + j, 0)),
        out_shape=jax.ShapeDtypeStruct((B * L, D_ATTN), BF16),
        scratch_shapes=[pltpu.VMEM((L, tq), I32), pltpu.VMEM((L, tq), F32),
                        pltpu.VMEM((D_ATTN, tq), F32),
                        pltpu.VMEM((N_HEADS, tq), F32), pltpu.VMEM((N_HEADS, tq), F32),
                        pltpu.VMEM((2, tq, tq), F32)],
        compiler_params=_cparams(("parallel", "arbitrary")),
    )(qT, kT, vT, qiT, ki, wT)


IDX_PAGES_PER_STEP = 16
ATT_PAGES_PER_STEP = 8
HEADS_PER_GROUP = 4


def _sample_index_kernel(pt_ref, q_ref, w_ref, kin_ref, *rest, n_pages_step):
    del pt_ref
    page_refs = rest[:n_pages_step]
    key_ref = rest[n_pages_step]
    g = pl.program_id(1)
    ng = pl.num_programs(1)
    n_tok = key_ref.shape[0]
    q = q_ref[...]
    w = w_ref[...]

    def scores(kT):
        d = jnp.dot(q, kT.astype(BF16), preferred_element_type=F32)
        r = jnp.maximum(d, 0.0) * w
        return jnp.sum(r.reshape(N_IDX_HEADS, n_tok, r.shape[1]), axis=0)

    for c in range(n_pages_step):
        col = pl.multiple_of((g * n_pages_step + c) * PAGE_SIZE, PAGE_SIZE)
        key_ref[:, pl.ds(col, PAGE_SIZE)] = _sortable_key(scores(page_refs[c][...]))

    @pl.when(g == ng - 1)
    def _():
        n_past = key_ref.shape[1] - PAGE_SIZE
        t = lax.broadcasted_iota(I32, (n_tok, PAGE_SIZE), 0)
        jn = lax.broadcasted_iota(I32, (n_tok, PAGE_SIZE), 1)
        key_ref[:, n_past:] = jnp.where(jn <= t, _sortable_key(scores(kin_ref[...])), INT_MIN)


def _sample_scores(page_table, q_rows, w_rows, kin_T, cache_kidx_T):
    Bs, n_pages = page_table.shape
    n = IDX_PAGES_PER_STEP
    T = q_rows.shape[1] // N_IDX_HEADS
    width = (n_pages + 1) * PAGE_SIZE
    per_seq = lambda b, g, pt: (b, 0, 0)
    page_specs = [pl.BlockSpec((None, IDX_DIM, PAGE_SIZE), lambda b, g, pt, c=c: (pt[b, g * n + c], 0, 0))
                  for c in range(n)]
    return pl.pallas_call(
        functools.partial(_sample_index_kernel, n_pages_step=n),
        name="sample_scores",
        grid_spec=pltpu.PrefetchScalarGridSpec(
            num_scalar_prefetch=1,
            grid=(Bs, n_pages // n),
            in_specs=[pl.BlockSpec((None,) + q_rows.shape[1:], per_seq),
                      pl.BlockSpec((None,) + w_rows.shape[1:], per_seq),
                      pl.BlockSpec((None, IDX_DIM, PAGE_SIZE), per_seq)] + page_specs,
            out_specs=pl.BlockSpec((None, T, width), per_seq)),
        out_shape=jax.ShapeDtypeStruct((Bs, T, width), I32),
        compiler_params=_cparams(("parallel", "arbitrary")),
    )(page_table, q_rows, w_rows, kin_T, *([cache_kidx_T] * n))


SELECT_ROWS = 64


def _select_bias_kernel(key_ref, bias_ref, *, topk):
    rows, width = key_ref.shape

    def count_ge(cand):
        return jnp.sum((key_ref[...] >= cand).astype(I32), axis=1, keepdims=True)

    thr = jnp.maximum(_kth_largest_key(count_ge, topk, (rows, 1)), INT_MIN + 1)
    n_ge = count_ge(thr)
    n_gt = count_ge(thr + 1)
    need = topk - n_gt
    bias_ref[...] = jnp.where(key_ref[...] >= thr, 0.0, NEG_BIG)

    @pl.when(jnp.max(n_ge - n_gt - need) > 0)
    def _():
        w = PAGE_SIZE
        tri = (lax.broadcasted_iota(I32, (w, w), 0) < lax.broadcasted_iota(I32, (w, w), 1)).astype(BF16)
        need_f = need.astype(F32)

        def tie_body(c, seen):
            c0 = pl.multiple_of(c * w, w)
            blk = key_ref[:, pl.ds(c0, w)]
            eq = blk == thr
            eq_f = jnp.where(eq, 1.0, 0.0)
            before = seen + jnp.dot(eq_f.astype(BF16), tri, preferred_element_type=F32)
            keep = (blk > thr) | (eq & (before < need_f))
            bias_ref[:, pl.ds(c0, w)] = jnp.where(keep, 0.0, NEG_BIG)
            return seen + jnp.sum(eq_f, axis=1, keepdims=True)

        lax.fori_loop(0, width // w, tie_body, jnp.zeros((rows, 1), F32))


def _select_bias(keys, topk):
    n_rows, width = keys.shape
    spec = pl.BlockSpec((SELECT_ROWS, width), lambda i: (i, 0))
    return pl.pallas_call(
        functools.partial(_select_bias_kernel, topk=topk),
        name="select_bias",
        grid=(n_rows // SELECT_ROWS,),
        in_specs=[spec],
        out_specs=spec,
        out_shape=jax.ShapeDtypeStruct((n_rows, width), F32),
        compiler_params=_cparams(("parallel",)),
    )(keys)


def _sample_attn_kernel(pt_ref, qbd_ref, bias_ref, biasn_ref, kn_ref, vn_ref, *rest, n_pages_step):
    del pt_ref
    k_refs, v_refs = rest[:n_pages_step], rest[n_pages_step:2 * n_pages_step]
    o_ref, m_scr, l_scr, acc_scr = rest[2 * n_pages_step:]
    g = pl.program_id(1)
    ng = pl.num_programs(1)
    n_tok = bias_ref.shape[0]
    hg = HEADS_PER_GROUP
    rows_g = hg * n_tok

    @pl.when(g == 0)
    def _():
        m_scr[...] = jnp.full(m_scr.shape, NEG_BIG, F32)
        l_scr[...] = jnp.zeros(l_scr.shape, F32)
        acc_scr[...] = jnp.zeros(acc_scr.shape, F32)

    def process(pages):
        n_grp = N_HEADS // hg

        def logits(grp):
            qg = qbd_ref[grp]
            out = []
            for k_ref, _, b in pages:
                kt = k_ref[grp * hg:(grp + 1) * hg].reshape(hg * HEAD_DIM, PAGE_SIZE).astype(BF16)
                s = jnp.dot(qg, kt, preferred_element_type=F32)
                out.append((s.reshape(hg, n_tok, PAGE_SIZE) + b[None]).reshape(rows_g, PAGE_SIZE))
            return out

        s_next = logits(0)
        for grp in range(n_grp):
            rows = slice(grp * rows_g, (grp + 1) * rows_g)
            s_list = s_next
            if grp + 1 < n_grp:
                s_next = logits(grp + 1)
            m_old = m_scr[rows, :]
            m_new = jnp.maximum(m_old, jnp.max(functools.reduce(jnp.maximum, s_list), axis=1, keepdims=True))
            alpha = jnp.exp(m_old - m_new)
            psum = jnp.zeros((rows_g, PAGE_SIZE), F32)
            pv = jnp.zeros((rows_g, hg * HEAD_DIM), F32)
            for s, (_, v_ref, _) in zip(s_list, pages):
                p = jnp.exp(s - m_new)
                psum = psum + p
                vt = v_ref[grp * hg:(grp + 1) * hg].reshape(hg * HEAD_DIM, PAGE_SIZE).astype(BF16)
                pv = pv + lax.dot_general(p.astype(BF16), vt, (((1,), (1,)), ((), ())),
                                          preferred_element_type=F32)
            l_scr[rows, :] = l_scr[rows, :] * alpha + jnp.sum(psum, axis=1, keepdims=True)
            acc_scr[rows, :] = acc_scr[rows, :] * alpha + pv
            m_scr[rows, :] = m_new

    process([(k_refs[c], v_refs[c], bias_ref[:, c * PAGE_SIZE:(c + 1) * PAGE_SIZE])
             for c in range(n_pages_step)])

    @pl.when(g == ng - 1)
    def _():
        process([(kn_ref, vn_ref, biasn_ref[...])])
        o_ref[...] = acc_scr[...] / l_scr[...]


def _sample_attention(page_table, qbd, bias, kT_new, vT_new, cache_kT, cache_vT):
    Bs, n_pages = page_table.shape
    n = ATT_PAGES_PER_STEP
    T = bias.shape[1]
    rows = N_HEADS * T
    per_seq4 = lambda b, g, pt: (b, 0, 0, 0)
    page_block = (None, N_HEADS, HEAD_DIM, PAGE_SIZE)
    page_specs = [pl.BlockSpec(page_block, lambda b, g, pt, c=c: (pt[b, g * n + c], 0, 0, 0)) for c in range(n)]
    return pl.pallas_call(
        functools.partial(_sample_attn_kernel, n_pages_step=n),
        name="sample_attn",
        grid_spec=pltpu.PrefetchScalarGridSpec(
            num_scalar_prefetch=1,
            grid=(Bs, n_pages // n),
            in_specs=[pl.BlockSpec((None,) + qbd.shape[1:], per_seq4),
                      pl.BlockSpec((None, T, n * PAGE_SIZE), lambda b, g, pt: (b, 0, g)),
                      pl.BlockSpec((None, T, PAGE_SIZE), lambda b, g, pt: (b, 0, n_pages)),
                      pl.BlockSpec(page_block, per_seq4),
                      pl.BlockSpec(page_block, per_seq4)] + page_specs + page_specs,
            out_specs=pl.BlockSpec((None, rows, HEADS_PER_GROUP * HEAD_DIM), lambda b, g, pt: (b, 0, 0)),
            scratch_shapes=[pltpu.VMEM((rows, 1), F32), pltpu.VMEM((rows, 1), F32),
                            pltpu.VMEM((rows, HEADS_PER_GROUP * HEAD_DIM), F32)]),
        out_shape=jax.ShapeDtypeStruct((Bs, rows, HEADS_PER_GROUP * HEAD_DIM), F32),
        compiler_params=_cparams(("parallel", "arbitrary")),
    )(page_table, qbd, bias, bias, kT_new, vT_new, *([cache_kT] * n), *([cache_vT] * n))


def _silu(x):
    return x * jax.nn.sigmoid(x)


def _rms(x):
    return lax.rsqrt(jnp.mean(x * x, axis=-1, keepdims=True) + EPS)


def _adaln(c_all, w_ada, b_ada):
    n = c_all.shape[0]
    n_pad = -(-n // 8) * 8
    c_pad = jnp.pad(c_all, ((0, n_pad - n), (0, 0)))
    tn = 1536
    (mod,) = _mm(c_pad, w_ada, trans_b=False, tm=n_pad, tn=tn, prologue=_silu,
                 extras=[(b_ada.reshape(1, -1), pl.BlockSpec((1, tn), lambda i, j: (0, j)))],
                 outs=[(jax.ShapeDtypeStruct((n_pad, w_ada.shape[1]), F32), _spec_mn(n_pad, tn))],
                 epilogue=lambda acc, b: (acc + b,))
    return mod[:n]


def _ki_wi_epilogue(acc, g):
    ki = acc[:, :IDX_DIM]
    wi = acc[:, IDX_DIM:IDX_DIM + N_IDX_HEADS] * (N_IDX_HEADS ** -0.5 * IDX_DIM ** -0.5)
    return ki * _rms(ki) * g, wi


def _small_proj(h, wT_small, idx_k_g, tm):
    M = h.shape[0]
    n = IDX_DIM + N_IDX_HEADS
    return _mm(h, wT_small, trans_b=True, tm=tm, tn=n,
               extras=[(idx_k_g.reshape(1, IDX_DIM), pl.BlockSpec((1, IDX_DIM), lambda i, j: (0, 0)))],
               outs=[(jax.ShapeDtypeStruct((M, IDX_DIM), F32), pl.BlockSpec((tm, IDX_DIM), lambda i, j: (i, 0))),
                     (jax.ShapeDtypeStruct((M, N_IDX_HEADS), F32),
                      pl.BlockSpec((tm, N_IDX_HEADS), lambda i, j: (i, 0)))],
               epilogue=_ki_wi_epilogue)


def _proj_T(wT, row_off, n_rows, h, n_seq, L, scale, want_f32):
    tr, tc = 512, 1024
    cps = L // tc
    spec = pl.BlockSpec((None, tr, tc), lambda i, j: (j // cps, i, j % cps))
    outs = [(jax.ShapeDtypeStruct((n_seq, n_rows, L), BF16), spec)]
    if want_f32:
        outs.append((jax.ShapeDtypeStruct((n_seq, n_rows, L), F32), spec))
    ep = (lambda acc: (acc * scale,) * len(outs)) if scale != 1.0 else (lambda acc: (acc,) * len(outs))
    return _mm(wT, h, trans_b=True, tm=tr, tn=tc, a_row_off=row_off // tr, m=n_rows, outs=outs, epilogue=ep)


FFN_CHUNK = 512


def _ffn_kernel(h2_ref, w1_ref, w2_ref, x1_ref, g2_ref, fg_ref, y_ref):
    f = pl.program_id(1)

    @pl.when(f == 0)
    def _():
        y_ref[...] = jnp.zeros(y_ref.shape, F32)

    a = jnp.dot(h2_ref[...], w1_ref[...].astype(BF16), preferred_element_type=F32)
    a = jnp.square(jnp.maximum(a, 0.0)).astype(BF16)
    for c0 in range(0, y_ref.shape[1], FFN_CHUNK):
        cols = slice(c0, c0 + FFN_CHUNK)
        y_ref[:, cols] += jnp.dot(a, w2_ref[:, cols].astype(BF16), preferred_element_type=F32)

    @pl.when(f == pl.num_programs(1) - 1)
    def _():
        x2 = x1_ref[...] + g2_ref[...] * y_ref[...]
        y_ref[...] = x2 * _rms(x2) * fg_ref[...]


def _tail(x_flat, m, w_o, w_ff1, w_ff2, norm2_g, final_g, g1, sc2, sh2, g2, tm_o, tm_f, rows_per_seq):
    M, D = x_flat.shape
    gvec = pl.BlockSpec((1, D), lambda i, j: (0, 0))
    tn_o = 512
    rv = _RowVec(tm_o, rows_per_seq)
    x1, h2 = _mm_rows(
        m, w_o, tm=tm_o, tn=tn_o,
        step_extras=[(x_flat, _spec_mn(tm_o, tn_o)), (g1, rv.cols(tn_o))],
        fin_extras=[(norm2_g.reshape(1, D), gvec), (sc2, rv.full(D)), (sh2, rv.full(D))],
        out_dtypes=[F32, BF16],
        step_fn=lambda d, x, g1_: x + g1_ * d,
        final_fn=lambda x1_, n2g, sc2_, sh2_: (None, (x1_ * _rms(x1_) * n2g) * (1.0 + sc2_) + sh2_))

    rv = _RowVec(tm_f, rows_per_seq)
    once = pl.Buffered(1)
    return pl.pallas_call(
        _ffn_kernel,
        name="ffn",
        grid=(M // tm_f, D_FF // FFN_CHUNK),
        in_specs=[pl.BlockSpec((tm_f, D), lambda i, f: (i, 0), pipeline_mode=once),
                  pl.BlockSpec((D, FFN_CHUNK), lambda i, f: (0, f)),
                  pl.BlockSpec((FFN_CHUNK, D), lambda i, f: (f, 0)),
                  pl.BlockSpec((tm_f, D), lambda i, f: (i, 0), pipeline_mode=once),
                  rv.full(D),
                  gvec],
        out_specs=pl.BlockSpec((tm_f, D), lambda i, f: (i, 0)),
        out_shape=jax.ShapeDtypeStruct((M, D), F32),
        compiler_params=_cparams(("parallel", "arbitrary")),
    )(h2, w_ff1, w_ff2, x1, g2, final_g.reshape(1, D))


def kernel(x_prompt, x_sample, cache_k, cache_v, cache_k_idx, state_conv, page_table,
           c_prompt, c_sample, w_ada, b_ada, norm1_g, norm2_g, w_in, conv_w, idx_k_g,
           w_bc, w_ba, w_o, w_ff1, w_ff2, final_g):
    B, L, D = x_prompt.shape
    Bs, T, _ = x_sample.shape
    Mp, Ms = B * L, Bs * T
    l = 0
    wT = jnp.transpose(w_in[l])
    wT_gates = wT[OFF_GCONV:]
    wT_small = wT[OFF_KI:OFF_GCONV]

    mod = _adaln(jnp.concatenate([c_prompt, c_sample], axis=0), w_ada[l], b_ada[l])
    mod_p = [v[:, None, :] for v in jnp.split(mod[:B], 6, axis=-1)]
    mod_s = [v[:, None, :] for v in jnp.split(mod[B:], 6, axis=-1)]

    sh1, sc1, g1, sh2, sc2, g2 = mod_p
    tm = 1024
    h = _norm_mod(x_prompt, norm1_g[l], sc1, sh1, tm)
    (zA,) = _mm(h, wT, trans_b=True, tm=tm, tn=512, n=3 * D_CONV,
                outs=[(jax.ShapeDtypeStruct((Mp, 3 * D_CONV), F32), _spec_mn(tm, 512))],
                epilogue=lambda acc: (acc,))
    (zG,) = _mm(h, wT_gates, trans_b=True, tm=tm, tn=512,
                outs=[(jax.ShapeDtypeStruct((Mp, 2 * D), F32), _spec_mn(tm, 512))],
                epilogue=lambda acc: (acc,))
    ki_p, wi_p = _small_proj(h, wT_small, idx_k_g[l], tm)
    (qT,) = _proj_T(wT, OFF_Q, D_ATTN, h, B, L, HEAD_DIM ** -0.5, False)
    kT, kT32 = _proj_T(wT, OFF_K, D_ATTN, h, B, L, 1.0, True)
    vT, vT32 = _proj_T(wT, OFF_V, D_ATTN, h, B, L, 1.0, True)
    (qiT,) = _proj_T(wT, OFF_QI, N_IDX_HEADS * IDX_DIM, h, B, L, 1.0, False)

    conv0 = jnp.zeros((B, CONV_W - 1, D_CONV), F32)
    yc, cv_p = _gated_conv(zA, conv0, conv_w[l], B, L, tm)
    wT_p = jnp.transpose(wi_p.reshape(B, L, N_IDX_HEADS), (0, 2, 1))
    ya = _prompt_attention(qT, kT, vT, qiT, ki_p.reshape(B, L, IDX_DIM), wT_p, min(TOPK_MAX, L // 4))
    m = _merge(yc, ya, w_bc[l], w_ba[l], zG, tm, 512)
    y_prompt = _tail(x_prompt.reshape(Mp, D), m, w_o[l], w_ff1[l], w_ff2[l], norm2_g[l], final_g,
                     g1, sc2, sh2, g2, 1024, 1024, L).reshape(B, L, D)
    k_prompt = jnp.transpose(kT32.reshape(B, N_HEADS, HEAD_DIM, L), (0, 3, 1, 2))
    v_prompt = jnp.transpose(vT32.reshape(B, N_HEADS, HEAD_DIM, L), (0, 3, 1, 2))
    kidx_prompt = ki_p.reshape(B, L, IDX_DIM)

    sh1, sc1, g1, sh2, sc2, g2 = mod_s
    hs = _norm_mod(x_sample, norm1_g[l], sc1, sh1, T)
    (zs,) = _mm(hs, wT, trans_b=True, tm=Ms, tn=512,
                outs=[(jax.ShapeDtypeStruct((Ms, N_IN), F32), _spec_mn(Ms, 512))],
                epilogue=lambda acc: (acc,))
    ki_s, wi_s = _small_proj(hs, wT_small, idx_k_g[l], Ms)
    ycs, cv_s = _gated_conv(zs, state_conv[l], conv_w[l], Bs, T, T)
    q_s = zs[:, OFF_Q:OFF_K].reshape(Bs, T, N_HEADS, HEAD_DIM)
    k_s = zs[:, OFF_K:OFF_V].reshape(Bs, T, N_HEADS, HEAD_DIM)
    v_s = zs[:, OFF_V:OFF_QI].reshape(Bs, T, N_HEADS, HEAD_DIM)
    qi_s = zs[:, OFF_QI:OFF_KI].reshape(Bs, T, N_IDX_HEADS, IDX_DIM)
    ki_s3 = ki_s.reshape(Bs, T, IDX_DIM)
    lane_pad = lambda a_: jnp.pad(a_, [(0, 0)] * (a_.ndim - 1) + [(0, PAGE_SIZE - T)])
    qi_rows = jnp.transpose(qi_s, (0, 2, 1, 3)).reshape(Bs, N_IDX_HEADS * T, IDX_DIM).astype(BF16)
    wi_rows = jnp.transpose(wi_s.reshape(Bs, T, N_IDX_HEADS), (0, 2, 1)).reshape(Bs, N_IDX_HEADS * T, 1)
    kin_T = lane_pad(jnp.transpose(ki_s3, (0, 2, 1)))
    n_grp = N_HEADS // HEADS_PER_GROUP
    qg = jnp.transpose((q_s * HEAD_DIM ** -0.5).reshape(Bs, T, n_grp, HEADS_PER_GROUP, HEAD_DIM), (0, 2, 3, 1, 4))
    qbd = jnp.einsum('bghtd,hk->bghtkd', qg, jnp.eye(HEADS_PER_GROUP, dtype=F32))
    qbd = qbd.reshape(Bs, n_grp, HEADS_PER_GROUP * T, HEADS_PER_GROUP * HEAD_DIM).astype(BF16)
    kT_new = lane_pad(jnp.transpose(k_s, (0, 2, 3, 1)))
    vT_new = lane_pad(jnp.transpose(v_s, (0, 2, 3, 1)))
    n_pages = page_table.shape[1]
    topk_s = min(TOPK_MAX, (n_pages * PAGE_SIZE + T) // 4)
    keys = _sample_scores(page_table, qi_rows, wi_rows, kin_T, jnp.transpose(cache_k_idx[l], (0, 2, 1)))
    bias = _select_bias(keys.reshape(Ms, -1), topk_s).reshape(Bs, T, -1)
    att = _sample_attention(page_table, qbd, bias, kT_new, vT_new,
                            jnp.transpose(cache_k[l], (0, 2, 3, 1)), jnp.transpose(cache_v[l], (0, 2, 3, 1)))
    att = att.reshape(Bs, n_grp, HEADS_PER_GROUP, T, HEADS_PER_GROUP, HEAD_DIM)
    att = jnp.diagonal(att, axis1=2, axis2=4)
    ya_s = jnp.transpose(att, (0, 2, 1, 4, 3)).reshape(Ms, D_ATTN).astype(BF16)
    m_s = _merge(ycs, ya_s, w_bc[l], w_ba[l], zs[:, OFF_GCONV:], Ms, 512)
    tok = lambda v: jnp.broadcast_to(v, (Bs, T, D)).reshape(Ms, D)
    y_sample = _tail(x_sample.reshape(Ms, D), m_s, w_o[l], w_ff1[l], w_ff2[l], norm2_g[l], final_g,
                     tok(g1), tok(sc2), tok(sh2), tok(g2), Ms, Ms, None).reshape(Bs, T, D)

    st = lambda a_: a_[None]
    return (y_prompt, y_sample, st(k_prompt), st(v_prompt), st(kidx_prompt), st(cv_p),
            st(k_s), st(v_s), st(ki_s3), st(cv_s))
```

```python
import functools

import jax
import jax.numpy as jnp
import numpy as np
from jax import lax
from jax.experimental import pallas as pl
from jax.experimental.pallas import tpu as pltpu

D_MODEL = 2048
PAGE_SIZE = 128
D_CONV = 1024
CONV_W = 3
N_HEADS = 16
HEAD_DIM = 64
D_ATTN = N_HEADS * HEAD_DIM
N_IDX_HEADS = 16
IDX_DIM = 64
TOPK_MAX = 256
D_FF = 4 * D_MODEL
EPS = 1e-6
IN_SIZES = (D_CONV, D_CONV, D_CONV, D_ATTN, D_ATTN, D_ATTN,
            N_IDX_HEADS * IDX_DIM, IDX_DIM, N_IDX_HEADS, D_MODEL, D_MODEL)
IN_OFFS = tuple(int(v) for v in np.cumsum((0,) + IN_SIZES))
OFF_U, OFF_GB, OFF_GC, OFF_Q, OFF_K, OFF_V, OFF_QI, OFF_KI, OFF_WI, OFF_GCONV, OFF_GATTN, N_IN = IN_OFFS

BF16 = jnp.bfloat16
F32 = jnp.float32
I32 = jnp.int32
INT_MIN = -2 ** 31
NEG_BIG = -1e30
VMEM_LIMIT_BYTES = 56 * 1024 * 1024
ATTN_BLOCK = 256


def _cparams(sem):
    return pltpu.CompilerParams(dimension_semantics=sem, vmem_limit_bytes=VMEM_LIMIT_BYTES)


def _mm_kernel(a_ref, b_ref, *refs, n_extra, trans_b, prologue, epilogue):
    extra, outs = refs[:n_extra], refs[n_extra:]
    a = a_ref[...]
    if prologue is not None:
        a = prologue(a)
    dims = (((1,), (1,)), ((), ())) if trans_b else (((1,), (0,)), ((), ()))
    d = lax.dot_general(a.astype(BF16), b_ref[...].astype(BF16), dims, preferred_element_type=F32)
    for o, r in zip(outs, epilogue(d, *[e[...] for e in extra])):
        o[...] = r.astype(o.dtype)


def _mm(a, b, *, trans_b, tm, tn, a_row_off=0, extras=(), outs, epilogue, prologue=None, m=None, n=None):
    kdim = a.shape[1]
    m = a.shape[0] if m is None else m
    if n is None:
        n = b.shape[0] if trans_b else b.shape[1]
    b_spec = pl.BlockSpec((tn, kdim), lambda i, j: (j, 0)) if trans_b else pl.BlockSpec((kdim, tn), lambda i, j: (0, j))
    return pl.pallas_call(
        functools.partial(_mm_kernel, n_extra=len(extras), trans_b=trans_b, prologue=prologue, epilogue=epilogue),
        grid=(pl.cdiv(m, tm), pl.cdiv(n, tn)),
        in_specs=[pl.BlockSpec((tm, kdim), lambda i, j: (i + a_row_off, 0)), b_spec] + [s for _, s in extras],
        out_specs=[s for _, s in outs],
        out_shape=[o for o, _ in outs],
        compiler_params=_cparams(("parallel", "arbitrary")),
    )(a, b, *[e for e, _ in extras])


def _mm_rows_kernel(a_ref, b_ref, *refs, n_step, n_fin, step_fn, final_fn):
    steps, fins, outs = refs[:n_step], refs[n_step:n_step + n_fin], refs[n_step + n_fin:]
    j = pl.program_id(1)
    tn = b_ref.shape[1]
    d = jnp.dot(a_ref[...], b_ref[...].astype(BF16), preferred_element_type=F32)
    outs[0][:, pl.ds(pl.multiple_of(j * tn, tn), tn)] = step_fn(d, *[e[...] for e in steps])

    @pl.when(j == pl.num_programs(1) - 1)
    def _():
        for o, r in zip(outs, final_fn(outs[0][...], *[f[...] for f in fins])):
            if r is not None:
                o[...] = r.astype(o.dtype)


def _mm_rows(a, b, *, tm, tn, step_extras, fin_extras, out_dtypes, step_fn, final_fn):
    M, kdim = a.shape
    N = b.shape[1]
    row = pl.BlockSpec((tm, N), lambda i, j: (i, 0))
    return pl.pallas_call(
        functools.partial(_mm_rows_kernel, n_step=len(step_extras), n_fin=len(fin_extras),
                          step_fn=step_fn, final_fn=final_fn),
        grid=(M // tm, N // tn),
        in_specs=[pl.BlockSpec((tm, kdim), lambda i, j: (i, 0)), pl.BlockSpec((kdim, tn), lambda i, j: (0, j))]
        + [s for _, s in step_extras] + [s for _, s in fin_extras],
        out_specs=[row] * len(out_dtypes),
        out_shape=[jax.ShapeDtypeStruct((M, N), dt) for dt in out_dtypes],
        compiler_params=_cparams(("parallel", "arbitrary")),
    )(a, b, *[e for e, _ in step_extras], *[e for e, _ in fin_extras])


def _spec_mn(tm, tn):
    return pl.BlockSpec((tm, tn), lambda i, j: (i, j))


class _RowVec:
    def __init__(self, tm, rows_per_seq):
        self.tm, self.bps = tm, (None if rows_per_seq is None else rows_per_seq // tm)

    def cols(self, tn):
        if self.bps is None:
            return pl.BlockSpec((self.tm, tn), lambda i, j: (i, j))
        return pl.BlockSpec((None, 1, tn), lambda i, j: (i // self.bps, 0, j))

    def full(self, n):
        if self.bps is None:
            return pl.BlockSpec((self.tm, n), lambda i, j: (i, 0))
        return pl.BlockSpec((None, 1, n), lambda i, j: (i // self.bps, 0, 0))


def _norm_mod_kernel(x_ref, g_ref, sc_ref, sh_ref, o_ref):
    x = x_ref[...]
    r = lax.rsqrt(jnp.mean(x * x, axis=-1, keepdims=True) + EPS)
    o_ref[...] = ((x * r * g_ref[...]) * (1.0 + sc_ref[...]) + sh_ref[...]).astype(o_ref.dtype)


def _norm_mod(x, g, sc, sh, tm):
    n_seq, L, D = x.shape
    bps = L // tm
    return pl.pallas_call(
        _norm_mod_kernel,
        grid=(n_seq, bps),
        in_specs=[pl.BlockSpec((None, tm, D), lambda s, i: (s, i, 0)),
                  pl.BlockSpec((1, D), lambda s, i: (0, 0)),
                  pl.BlockSpec((None, 1, D), lambda s, i: (s, 0, 0)),
                  pl.BlockSpec((None, 1, D), lambda s, i: (s, 0, 0))],
        out_specs=pl.BlockSpec((tm, D), lambda s, i: (s * bps + i, 0)),
        out_shape=jax.ShapeDtypeStruct((n_seq * L, D), BF16),
        compiler_params=_cparams(("parallel", "arbitrary")),
    )(x, g.reshape(1, D), sc, sh)


def _conv_kernel(u_ref, gb_ref, gc_ref, prev_ref, w_ref, y_ref, new_ref, carry_ref):
    i = pl.program_id(1)
    tm = u_ref.shape[0]

    @pl.when(i == 0)
    def _():
        carry_ref[...] = prev_ref[...]

    cu = gc_ref[...] * u_ref[...]
    row = lax.broadcasted_iota(I32, cu.shape, 0)
    c0 = carry_ref[0:1, :]
    c1 = carry_ref[1:2, :]
    s1 = jnp.where(row == 0, c1, pltpu.roll(cu, 1, 0))
    s2 = jnp.where(row == 0, c0, jnp.where(row == 1, c1, pltpu.roll(cu, 2, 0)))
    conv = w_ref[0:1, :] * s2 + w_ref[1:2, :] * s1 + w_ref[2:3, :] * cu
    y_ref[...] = (gb_ref[...] * conv).astype(y_ref.dtype)
    last2 = cu[tm - 2:tm, :]
    carry_ref[...] = last2
    new_ref[...] = last2


def _gated_conv(z, prev, conv_w, n_seq, L, tm):
    C = D_CONV
    bps = L // tm
    row = lambda s, i: s * bps + i
    return pl.pallas_call(
        _conv_kernel,
        grid=(n_seq, bps),
        in_specs=[pl.BlockSpec((tm, C), lambda s, i: (row(s, i), 0)),
                  pl.BlockSpec((tm, C), lambda s, i: (row(s, i), 1)),
                  pl.BlockSpec((tm, C), lambda s, i: (row(s, i), 2)),
                  pl.BlockSpec((None, 2, C), lambda s, i: (s, 0, 0)),
                  pl.BlockSpec((CONV_W, C), lambda s, i: (0, 0))],
        out_specs=[pl.BlockSpec((tm, C), lambda s, i: (row(s, i), 0)),
                   pl.BlockSpec((None, 2, C), lambda s, i: (s, 0, 0))],
        out_shape=[jax.ShapeDtypeStruct((n_seq * L, C), BF16),
                   jax.ShapeDtypeStruct((n_seq, 2, C), F32)],
        scratch_shapes=[pltpu.VMEM((2, C), F32)],
        compiler_params=_cparams(("arbitrary", "arbitrary")),
    )(z, z, z, prev, conv_w)


def _merge_kernel(yc_ref, ya_ref, wbc_ref, wba_ref, gc_ref, ga_ref, o_ref):
    dc = jnp.dot(yc_ref[...], wbc_ref[...].astype(BF16), preferred_element_type=F32)
    da = jnp.dot(ya_ref[...], wba_ref[...].astype(BF16), preferred_element_type=F32)
    o_ref[...] = (jax.nn.sigmoid(gc_ref[...]) * dc + jax.nn.sigmoid(ga_ref[...]) * da).astype(o_ref.dtype)


def _merge(yc, ya, w_bc, w_ba, zg, tm, tn):
    M = yc.shape[0]
    nb = D_MODEL // tn
    return pl.pallas_call(
        _merge_kernel,
        grid=(M // tm, nb),
        in_specs=[pl.BlockSpec((tm, D_CONV), lambda i, j: (i, 0)),
                  pl.BlockSpec((tm, D_ATTN), lambda i, j: (i, 0)),
                  pl.BlockSpec((D_CONV, tn), lambda i, j: (0, j)),
                  pl.BlockSpec((D_ATTN, tn), lambda i, j: (0, j)),
                  pl.BlockSpec((tm, tn), lambda i, j: (i, j)),
                  pl.BlockSpec((tm, tn), lambda i, j: (i, j + nb))],
        out_specs=pl.BlockSpec((tm, tn), lambda i, j: (i, j)),
        out_shape=jax.ShapeDtypeStruct((M, D_MODEL), BF16),
        compiler_params=_cparams(("parallel", "arbitrary")),
    )(yc, ya, w_bc, w_ba, zg, zg)


NEG_INF = float("-inf")
LOWEST_KEY = INT_MIN + 0x00800000


def _key_to_float(key):
    return lax.bitcast_convert_type(key ^ ((key >> 31) & 0x7FFFFFFF), F32)


def _kth_largest(count, k, shape):
    n_ge = lambda key: count(_key_to_float(key), False)
    base = jnp.where(n_ge(jnp.zeros(shape, I32)) >= k, 0, INT_MIN).astype(I32)

    def bit_body(t, base):
        cand = base + jnp.left_shift(jnp.int32(1), 30 - t)
        return jnp.where(n_ge(cand) >= k, cand, base)

    return _key_to_float(jnp.maximum(lax.fori_loop(0, 31, bit_body, base), LOWEST_KEY))


def _prompt_attn_kernel(qT_ref, kT_ref, vT_ref, qiT_ref, ki_ref, wT_ref, o_ref,
                        key_scr, bias_scr, outT_scr, m_scr, l_scr, s_scr, *, topk):
    j = pl.program_id(1)
    tq = qT_ref.shape[1]
    ck = tq
    sub = ck // 2

    def score_body(c, _):
        k0 = pl.multiple_of(c * sub, sub)
        ki_c = ki_ref[pl.ds(k0, sub), :].astype(BF16)
        acc = jnp.zeros((sub, tq), F32)
        for h in range(N_IDX_HEADS):
            d = jnp.dot(ki_c, qiT_ref[h * IDX_DIM:(h + 1) * IDX_DIM, :], preferred_element_type=F32)
            acc = acc + jnp.maximum(d, 0.0) * wT_ref[h:h + 1, :]
        kpos = k0 + lax.broadcasted_iota(I32, (sub, tq), 0)
        qpos = j * tq + lax.broadcasted_iota(I32, (sub, tq), 1)
        key_scr[pl.ds(k0, sub), :] = jnp.where(kpos <= qpos, acc, NEG_INF)
        return 0

    lax.fori_loop(0, 2 * (j + 1), score_body, 0)

    def count(v, strict):
        def body(c, cnt):
            blk = key_scr[pl.ds(pl.multiple_of(c * ck, ck), ck), :]
            hit = (blk > v) if strict else (blk >= v)
            return cnt + jnp.sum(hit.astype(I32), axis=0, keepdims=True)
        return lax.fori_loop(0, j + 1, body, jnp.zeros((1, tq), I32))

    thr = _kth_largest(count, topk, (1, tq))
    n_ge = count(thr, False)
    n_gt = count(thr, True)
    need = topk - n_gt

    def bias_body(c, _):
        k0 = pl.multiple_of(c * ck, ck)
        bias_scr[pl.ds(k0, ck), :] = jnp.where(key_scr[pl.ds(k0, ck), :] >= thr, 0.0, NEG_BIG)
        return 0

    lax.fori_loop(0, j + 1, bias_body, 0)

    @pl.when(jnp.max(n_ge - n_gt - need) > 0)
    def _():
        tri = (lax.broadcasted_iota(I32, (ck, ck), 1) < lax.broadcasted_iota(I32, (ck, ck), 0)).astype(BF16)
        need_f = need.astype(F32)

        def tie_body(c, seen):
            k0 = pl.multiple_of(c * ck, ck)
            blk = key_scr[pl.ds(k0, ck), :]
            eq = blk == thr
            eq_f = jnp.where(eq, 1.0, 0.0)
            before = seen + jnp.dot(tri, eq_f.astype(BF16), preferred_element_type=F32)
            keep = (blk > thr) | (eq & (before < need_f))
            bias_scr[pl.ds(k0, ck), :] = jnp.where(keep, 0.0, NEG_BIG)
            return seen + jnp.sum(eq_f, axis=0, keepdims=True)

        lax.fori_loop(0, j + 1, tie_body, jnp.zeros((1, tq), F32))

    m_scr[...] = jnp.full(m_scr.shape, NEG_BIG, F32)
    l_scr[...] = jnp.zeros(l_scr.shape, F32)
    outT_scr[...] = jnp.zeros(outT_scr.shape, F32)

    def chunk_body(c, _):
        k0 = pl.multiple_of(c * ck, ck)

        def logits(h):
            r = slice(h * HEAD_DIM, (h + 1) * HEAD_DIM)
            return lax.dot_general(kT_ref[r, pl.ds(k0, ck)], qT_ref[r, :], (((0,), (0,)), ((), ())),
                                   preferred_element_type=F32)

        s_scr[0] = logits(0)
        for h in range(N_HEADS):
            r = slice(h * HEAD_DIM, (h + 1) * HEAD_DIM)
            if h + 1 < N_HEADS:
                s_scr[(h + 1) % 2] = logits(h + 1)
            s = s_scr[h % 2] + bias_scr[pl.ds(k0, ck), :]
            m_old = m_scr[h:h + 1, :]
            m_new = jnp.maximum(m_old, jnp.max(s, axis=0, keepdims=True))
            p = jnp.exp(s - m_new)
            alpha = jnp.exp(m_old - m_new)
            l_scr[h:h + 1, :] = l_scr[h:h + 1, :] * alpha + jnp.sum(p, axis=0, keepdims=True)
            pv = jnp.dot(vT_ref[r, pl.ds(k0, ck)], p.astype(BF16), preferred_element_type=F32)
            outT_scr[r, :] = outT_scr[r, :] * alpha + pv
            m_scr[h:h + 1, :] = m_new
        return 0

    lax.fori_loop(0, j + 1, chunk_body, 0)
    out = outT_scr[...].reshape(N_HEADS, HEAD_DIM, tq) / l_scr[...][:, None, :]
    o_ref[...] = out.reshape(N_HEADS * HEAD_DIM, tq).T.astype(o_ref.dtype)


def _prompt_attention(qT, kT, vT, qiT, ki, wT, topk):
    B, _, L = qT.shape
    tq = ATTN_BLOCK
    nq = L // tq
    feat = lambda b, j: (b, 0, j)
    whole = lambda b, j: (b, 0, 0)
    return pl.pallas_call(
        functools.partial(_prompt_attn_kernel, topk=topk),
        name="prompt_attn",
        grid=(B, nq),
        in_specs=[pl.BlockSpec((None, D_ATTN, tq), feat),
                  pl.BlockSpec((None, D_ATTN, L), whole),
                  pl.BlockSpec((None, D_ATTN, L), whole),
                  pl.BlockSpec((None, N_IDX_HEADS * IDX_DIM, tq), feat),
                  pl.BlockSpec((None, L, IDX_DIM), whole),
                  pl.BlockSpec((None, N_IDX_HEADS, tq), feat)],
        out_specs=pl.BlockSpec((tq, D_ATTN), lambda b, j: (b * nq + j, 0)),
        out_shape=jax.ShapeDtypeStruct((B * L, D_ATTN), BF16),
        scratch_shapes=[pltpu.VMEM((L, tq), F32), pltpu.VMEM((L, tq), F32),
                        pltpu.VMEM((D_ATTN, tq), F32),
                        pltpu.VMEM((N_HEADS, tq), F32), pltpu.VMEM((N_HEADS, tq), F32),
                        pltpu.VMEM((2, tq, tq), F32)],
        compiler_params=_cparams(("parallel", "arbitrary")),
    )(qT, kT, vT, qiT, ki, wT)


IDX_PAGES_PER_STEP = 32
ATT_PAGES_PER_STEP = 16
PAGES_PER_TILE = 2
HEADS_PER_GROUP = 4


def _sample_index_kernel(pt_ref, q_ref, w_ref, kin_ref, *rest, n_pages_step):
    del pt_ref
    page_refs = rest[:n_pages_step]
    key_ref = rest[n_pages_step]
    g = pl.program_id(1)
    ng = pl.num_programs(1)
    n_tok = key_ref.shape[0]
    q = q_ref[...]
    w = w_ref[...]

    def scores(kT):
        d = jnp.dot(q, kT.astype(BF16), preferred_element_type=F32)
        r = jnp.maximum(d, 0.0) * w
        return jnp.sum(r.reshape(N_IDX_HEADS, n_tok, r.shape[1]), axis=0)

    npt = PAGES_PER_TILE
    for c in range(0, n_pages_step, npt):
        col = pl.multiple_of((g * n_pages_step + c) * PAGE_SIZE, npt * PAGE_SIZE)
        kT = jnp.concatenate([page_refs[c + i][...] for i in range(npt)], axis=1)
        key_ref[:, pl.ds(col, npt * PAGE_SIZE)] = scores(kT)

    @pl.when(g == ng - 1)
    def _():
        n_past = key_ref.shape[1] - PAGE_SIZE
        t = lax.broadcasted_iota(I32, (n_tok, PAGE_SIZE), 0)
        jn = lax.broadcasted_iota(I32, (n_tok, PAGE_SIZE), 1)
        key_ref[:, n_past:] = jnp.where(jn <= t, scores(kin_ref[...]), NEG_INF)


def _sample_scores(page_table, q_rows, w_rows, kin_T, cache_kidx_T):
    Bs, n_pages = page_table.shape
    n = IDX_PAGES_PER_STEP
    T = q_rows.shape[1] // N_IDX_HEADS
    width = (n_pages + 1) * PAGE_SIZE
    per_seq = lambda b, g, pt: (b, 0, 0)
    page_specs = [pl.BlockSpec((None, IDX_DIM, PAGE_SIZE), lambda b, g, pt, c=c: (pt[b, g * n + c], 0, 0))
                  for c in range(n)]
    return pl.pallas_call(
        functools.partial(_sample_index_kernel, n_pages_step=n),
        name="sample_scores",
        grid_spec=pltpu.PrefetchScalarGridSpec(
            num_scalar_prefetch=1,
            grid=(Bs, n_pages // n),
            in_specs=[pl.BlockSpec((None,) + q_rows.shape[1:], per_seq),
                      pl.BlockSpec((None,) + w_rows.shape[1:], per_seq),
                      pl.BlockSpec((None, IDX_DIM, PAGE_SIZE), per_seq)] + page_specs,
            out_specs=pl.BlockSpec((None, T, width), per_seq)),
        out_shape=jax.ShapeDtypeStruct((Bs, T, width), F32),
        compiler_params=_cparams(("parallel", "arbitrary")),
    )(page_table, q_rows, w_rows, kin_T, *([cache_kidx_T] * n))


SELECT_ROWS = 64


def _select_bias_kernel(key_ref, bias_ref, *, topk):
    rows, width = key_ref.shape

    def count(v, strict):
        hit = (key_ref[...] > v) if strict else (key_ref[...] >= v)
        return jnp.sum(hit.astype(I32), axis=1, keepdims=True)

    thr = _kth_largest(count, topk, (rows, 1))
    n_ge = count(thr, False)
    n_gt = count(thr, True)
    need = topk - n_gt
    bias_ref[...] = jnp.where(key_ref[...] >= thr, 0.0, NEG_BIG)

    @pl.when(jnp.max(n_ge - n_gt - need) > 0)
    def _():
        w = PAGE_SIZE
        tri = (lax.broadcasted_iota(I32, (w, w), 0) < lax.broadcasted_iota(I32, (w, w), 1)).astype(BF16)
        need_f = need.astype(F32)

        def tie_body(c, seen):
            c0 = pl.multiple_of(c * w, w)
            blk = key_ref[:, pl.ds(c0, w)]
            eq = blk == thr
            eq_f = jnp.where(eq, 1.0, 0.0)
            before = seen + jnp.dot(eq_f.astype(BF16), tri, preferred_element_type=F32)
            keep = (blk > thr) | (eq & (before < need_f))
            bias_ref[:, pl.ds(c0, w)] = jnp.where(keep, 0.0, NEG_BIG)
            return seen + jnp.sum(eq_f, axis=1, keepdims=True)

        lax.fori_loop(0, width // w, tie_body, jnp.zeros((rows, 1), F32))


def _select_bias(keys, topk):
    n_rows, width = keys.shape
    spec = pl.BlockSpec((SELECT_ROWS, width), lambda i: (i, 0))
    return pl.pallas_call(
        functools.partial(_select_bias_kernel, topk=topk),
        name="select_bias",
        grid=(n_rows // SELECT_ROWS,),
        in_specs=[spec],
        out_specs=spec,
        out_shape=jax.ShapeDtypeStruct((n_rows, width), F32),
        compiler_params=_cparams(("parallel",)),
    )(keys)


def _sample_attn_kernel(pt_ref, qbd_ref, bias_ref, biasn_ref, kn_ref, vn_ref, *rest, n_pages_step):
    del pt_ref
    k_refs, v_refs = rest[:n_pages_step], rest[n_pages_step:2 * n_pages_step]
    o_ref, m_scr, l_scr, acc_scr = rest[2 * n_pages_step:]
    g = pl.program_id(1)
    ng = pl.num_programs(1)
    n_tok = bias_ref.shape[0]
    hg = HEADS_PER_GROUP
    rows_g = hg * n_tok

    @pl.when(g == 0)
    def _():
        m_scr[...] = jnp.full(m_scr.shape, NEG_BIG, F32)
        l_scr[...] = jnp.zeros(l_scr.shape, F32)
        acc_scr[...] = jnp.zeros(acc_scr.shape, F32)

    def process(tiles):
        n_grp = N_HEADS // hg

        def side_by_side(refs, grp):
            return jnp.concatenate(
                [r[grp * hg:(grp + 1) * hg].reshape(hg * HEAD_DIM, PAGE_SIZE).astype(BF16) for r in refs], axis=1)

        def logits(grp):
            qg = qbd_ref[grp]
            out = []
            for k_refs_t, _, b in tiles:
                s = jnp.dot(qg, side_by_side(k_refs_t, grp), preferred_element_type=F32)
                out.append((s.reshape(hg, n_tok, s.shape[1]) + b[None]).reshape(rows_g, s.shape[1]))
            return out

        s_next = logits(0)
        for grp in range(n_grp):
            rows = slice(grp * rows_g, (grp + 1) * rows_g)
            s_list = s_next
            if grp + 1 < n_grp:
                s_next = logits(grp + 1)
            m_old = m_scr[rows, :]
            m_new = jnp.maximum(m_old, jnp.max(functools.reduce(jnp.maximum, s_list), axis=1, keepdims=True))
            alpha = jnp.exp(m_old - m_new)
            psum = jnp.zeros(s_list[0].shape, F32)
            pv = jnp.zeros((rows_g, hg * HEAD_DIM), F32)
            for s, (_, v_refs_t, _) in zip(s_list, tiles):
                p = jnp.exp(s - m_new)
                psum = psum + p
                pv = pv + lax.dot_general(p.astype(BF16), side_by_side(v_refs_t, grp), (((1,), (1,)), ((), ())),
                                          preferred_element_type=F32)
            l_scr[rows, :] = l_scr[rows, :] * alpha + jnp.sum(psum, axis=1, keepdims=True)
            acc_scr[rows, :] = acc_scr[rows, :] * alpha + pv
            m_scr[rows, :] = m_new

    npt = PAGES_PER_TILE
    process([(k_refs[c:c + npt], v_refs[c:c + npt], bias_ref[:, c * PAGE_SIZE:(c + npt) * PAGE_SIZE])
             for c in range(0, n_pages_step, npt)])

    @pl.when(g == ng - 1)
    def _():
        process([((kn_ref,), (vn_ref,), biasn_ref[...])])
        o_ref[...] = acc_scr[...] / l_scr[...]


def _sample_attention(page_table, qbd, bias, kT_new, vT_new, cache_kT, cache_vT):
    Bs, n_pages = page_table.shape
    n = ATT_PAGES_PER_STEP
    T = bias.shape[1]
    rows = N_HEADS * T
    per_seq4 = lambda b, g, pt: (b, 0, 0, 0)
    page_block = (None, N_HEADS, HEAD_DIM, PAGE_SIZE)
    page_specs = [pl.BlockSpec(page_block, lambda b, g, pt, c=c: (pt[b, g * n + c], 0, 0, 0)) for c in range(n)]
    return pl.pallas_call(
        functools.partial(_sample_attn_kernel, n_pages_step=n),
        name="sample_attn",
        grid_spec=pltpu.PrefetchScalarGridSpec(
            num_scalar_prefetch=1,
            grid=(Bs, n_pages // n),
            in_specs=[pl.BlockSpec((None,) + qbd.shape[1:], per_seq4),
                      pl.BlockSpec((None, T, n * PAGE_SIZE), lambda b, g, pt: (b, 0, g)),
                      pl.BlockSpec((None, T, PAGE_SIZE), lambda b, g, pt: (b, 0, n_pages)),
                      pl.BlockSpec(page_block, per_seq4),
                      pl.BlockSpec(page_block, per_seq4)] + page_specs + page_specs,
            out_specs=pl.BlockSpec((None, rows, HEADS_PER_GROUP * HEAD_DIM), lambda b, g, pt: (b, 0, 0)),
            scratch_shapes=[pltpu.VMEM((rows, 1), F32), pltpu.VMEM((rows, 1), F32),
                            pltpu.VMEM((rows, HEADS_PER_GROUP * HEAD_DIM), F32)]),
        out_shape=jax.ShapeDtypeStruct((Bs, rows, HEADS_PER_GROUP * HEAD_DIM), F32),
        compiler_params=_cparams(("parallel", "arbitrary")),
    )(page_table, qbd, bias, bias, kT_new, vT_new, *([cache_kT] * n), *([cache_vT] * n))


def _silu(x):
    return x * jax.nn.sigmoid(x)


def _rms(x):
    return lax.rsqrt(jnp.mean(x * x, axis=-1, keepdims=True) + EPS)


def _adaln(c_all, w_ada, b_ada):
    n = c_all.shape[0]
    n_pad = -(-n // 8) * 8
    c_pad = jnp.pad(c_all, ((0, n_pad - n), (0, 0)))
    tn = 1536
    (mod,) = _mm(c_pad, w_ada, trans_b=False, tm=n_pad, tn=tn, prologue=_silu,
                 extras=[(b_ada.reshape(1, -1), pl.BlockSpec((1, tn), lambda i, j: (0, j)))],
                 outs=[(jax.ShapeDtypeStruct((n_pad, w_ada.shape[1]), F32), _spec_mn(n_pad, tn))],
                 epilogue=lambda acc, b: (acc + b,))
    return mod[:n]


def _ki_wi_epilogue(acc, g):
    ki = acc[:, :IDX_DIM]
    wi = acc[:, IDX_DIM:IDX_DIM + N_IDX_HEADS] * (N_IDX_HEADS ** -0.5 * IDX_DIM ** -0.5)
    return ki * _rms(ki) * g, wi


def _small_proj(h, wT_small, idx_k_g, tm):
    M = h.shape[0]
    n = IDX_DIM + N_IDX_HEADS
    return _mm(h, wT_small, trans_b=True, tm=tm, tn=n,
               extras=[(idx_k_g.reshape(1, IDX_DIM), pl.BlockSpec((1, IDX_DIM), lambda i, j: (0, 0)))],
               outs=[(jax.ShapeDtypeStruct((M, IDX_DIM), F32), pl.BlockSpec((tm, IDX_DIM), lambda i, j: (i, 0))),
                     (jax.ShapeDtypeStruct((M, N_IDX_HEADS), F32),
                      pl.BlockSpec((tm, N_IDX_HEADS), lambda i, j: (i, 0)))],
               epilogue=_ki_wi_epilogue)


def _proj_T(wT, row_off, n_rows, h, n_seq, L, scale, want_f32):
    tr, tc = 512, 1024
    cps = L // tc
    spec = pl.BlockSpec((None, tr, tc), lambda i, j: (j // cps, i, j % cps))
    outs = [(jax.ShapeDtypeStruct((n_seq, n_rows, L), BF16), spec)]
    if want_f32:
        outs.append((jax.ShapeDtypeStruct((n_seq, n_rows, L), F32), spec))
    ep = (lambda acc: (acc * scale,) * len(outs)) if scale != 1.0 else (lambda acc: (acc,) * len(outs))
    return _mm(wT, h, trans_b=True, tm=tr, tn=tc, a_row_off=row_off // tr, m=n_rows, outs=outs, epilogue=ep)


FFN_CHUNK = 512


def _ffn_kernel(h2_ref, w1_ref, w2_ref, x1_ref, g2_ref, fg_ref, y_ref):
    f = pl.program_id(1)

    @pl.when(f == 0)
    def _():
        y_ref[...] = jnp.zeros(y_ref.shape, F32)

    a = jnp.dot(h2_ref[...], w1_ref[...].astype(BF16), preferred_element_type=F32)
    a = jnp.square(jnp.maximum(a, 0.0)).astype(BF16)
    for c0 in range(0, y_ref.shape[1], FFN_CHUNK):
        cols = slice(c0, c0 + FFN_CHUNK)
        y_ref[:, cols] += jnp.dot(a, w2_ref[:, cols].astype(BF16), preferred_element_type=F32)

    @pl.when(f == pl.num_programs(1) - 1)
    def _():
        x2 = x1_ref[...] + g2_ref[...] * y_ref[...]
        y_ref[...] = x2 * _rms(x2) * fg_ref[...]


def _tail(x_flat, m, w_o, w_ff1, w_ff2, norm2_g, final_g, g1, sc2, sh2, g2, tm_o, tm_f, rows_per_seq):
    M, D = x_flat.shape
    gvec = pl.BlockSpec((1, D), lambda i, j: (0, 0))
    tn_o = 512
    rv = _RowVec(tm_o, rows_per_seq)
    x1, h2 = _mm_rows(
        m, w_o, tm=tm_o, tn=tn_o,
        step_extras=[(x_flat, _spec_mn(tm_o, tn_o)), (g1, rv.cols(tn_o))],
        fin_extras=[(norm2_g.reshape(1, D), gvec), (sc2, rv.full(D)), (sh2, rv.full(D))],
        out_dtypes=[F32, BF16],
        step_fn=lambda d, x, g1_: x + g1_ * d,
        final_fn=lambda x1_, n2g, sc2_, sh2_: (None, (x1_ * _rms(x1_) * n2g) * (1.0 + sc2_) + sh2_))

    rv = _RowVec(tm_f, rows_per_seq)
    once = pl.Buffered(1)
    return pl.pallas_call(
        _ffn_kernel,
        name="ffn",
        grid=(M // tm_f, D_FF // FFN_CHUNK),
        in_specs=[pl.BlockSpec((tm_f, D), lambda i, f: (i, 0), pipeline_mode=once),
                  pl.BlockSpec((D, FFN_CHUNK), lambda i, f: (0, f)),
                  pl.BlockSpec((FFN_CHUNK, D), lambda i, f: (f, 0)),
                  pl.BlockSpec((tm_f, D), lambda i, f: (i, 0), pipeline_mode=once),
                  rv.full(D),
                  gvec],
        out_specs=pl.BlockSpec((tm_f, D), lambda i, f: (i, 0)),
        out_shape=jax.ShapeDtypeStruct((M, D), F32),
        compiler_params=_cparams(("parallel", "arbitrary")),
    )(h2, w_ff1, w_ff2, x1, g2, final_g.reshape(1, D))


def kernel(x_prompt, x_sample, cache_k, cache_v, cache_k_idx, state_conv, page_table,
           c_prompt, c_sample, w_ada, b_ada, norm1_g, norm2_g, w_in, conv_w, idx_k_g,
           w_bc, w_ba, w_o, w_ff1, w_ff2, final_g):
    B, L, D = x_prompt.shape
    Bs, T, _ = x_sample.shape
    Mp, Ms = B * L, Bs * T
    l = 0
    wT = jnp.transpose(w_in[l])
    wT_gates = wT[OFF_GCONV:]
    wT_small = wT[OFF_KI:OFF_GCONV]

    mod = _adaln(jnp.concatenate([c_prompt, c_sample], axis=0), w_ada[l], b_ada[l])
    mod_p = [v[:, None, :] for v in jnp.split(mod[:B], 6, axis=-1)]
    mod_s = [v[:, None, :] for v in jnp.split(mod[B:], 6, axis=-1)]

    sh1, sc1, g1, sh2, sc2, g2 = mod_p
    tm = 1024
    h = _norm_mod(x_prompt, norm1_g[l], sc1, sh1, tm)
    (zA,) = _mm(h, wT, trans_b=True, tm=tm, tn=512, n=3 * D_CONV,
                outs=[(jax.ShapeDtypeStruct((Mp, 3 * D_CONV), F32), _spec_mn(tm, 512))],
                epilogue=lambda acc: (acc,))
    (zG,) = _mm(h, wT_gates, trans_b=True, tm=tm, tn=512,
                outs=[(jax.ShapeDtypeStruct((Mp, 2 * D), F32), _spec_mn(tm, 512))],
                epilogue=lambda acc: (acc,))
    ki_p, wi_p = _small_proj(h, wT_small, idx_k_g[l], tm)
    (qT,) = _proj_T(wT, OFF_Q, D_ATTN, h, B, L, HEAD_DIM ** -0.5, False)
    kT, kT32 = _proj_T(wT, OFF_K, D_ATTN, h, B, L, 1.0, True)
    vT, vT32 = _proj_T(wT, OFF_V, D_ATTN, h, B, L, 1.0, True)
    (qiT,) = _proj_T(wT, OFF_QI, N_IDX_HEADS * IDX_DIM, h, B, L, 1.0, False)

    conv0 = jnp.zeros((B, CONV_W - 1, D_CONV), F32)
    yc, cv_p = _gated_conv(zA, conv0, conv_w[l], B, L, tm)
    wT_p = jnp.transpose(wi_p.reshape(B, L, N_IDX_HEADS), (0, 2, 1))
    ya = _prompt_attention(qT, kT, vT, qiT, ki_p.reshape(B, L, IDX_DIM), wT_p, min(TOPK_MAX, L // 4))
    m = _merge(yc, ya, w_bc[l], w_ba[l], zG, tm, 1024)
    y_prompt = _tail(x_prompt.reshape(Mp, D), m, w_o[l], w_ff1[l], w_ff2[l], norm2_g[l], final_g,
                     g1, sc2, sh2, g2, 1024, 1024, L).reshape(B, L, D)
    k_prompt = jnp.transpose(kT32.reshape(B, N_HEADS, HEAD_DIM, L), (0, 3, 1, 2))
    v_prompt = jnp.transpose(vT32.reshape(B, N_HEADS, HEAD_DIM, L), (0, 3, 1, 2))
    kidx_prompt = ki_p.reshape(B, L, IDX_DIM)

    sh1, sc1, g1, sh2, sc2, g2 = mod_s
    hs = _norm_mod(x_sample, norm1_g[l], sc1, sh1, T)
    (zs,) = _mm(hs, wT, trans_b=True, tm=Ms, tn=512,
                outs=[(jax.ShapeDtypeStruct((Ms, N_IN), F32), _spec_mn(Ms, 512))],
                epilogue=lambda acc: (acc,))
    ki_s, wi_s = _small_proj(hs, wT_small, idx_k_g[l], Ms)
    ycs, cv_s = _gated_conv(zs, state_conv[l], conv_w[l], Bs, T, T)
    q_s = zs[:, OFF_Q:OFF_K].reshape(Bs, T, N_HEADS, HEAD_DIM)
    k_s = zs[:, OFF_K:OFF_V].reshape(Bs, T, N_HEADS, HEAD_DIM)
    v_s = zs[:, OFF_V:OFF_QI].reshape(Bs, T, N_HEADS, HEAD_DIM)
    qi_s = zs[:, OFF_QI:OFF_KI].reshape(Bs, T, N_IDX_HEADS, IDX_DIM)
    ki_s3 = ki_s.reshape(Bs, T, IDX_DIM)
    lane_pad = lambda a_: jnp.pad(a_, [(0, 0)] * (a_.ndim - 1) + [(0, PAGE_SIZE - T)])
    qi_rows = jnp.transpose(qi_s, (0, 2, 1, 3)).reshape(Bs, N_IDX_HEADS * T, IDX_DIM).astype(BF16)
    wi_rows = jnp.transpose(wi_s.reshape(Bs, T, N_IDX_HEADS), (0, 2, 1)).reshape(Bs, N_IDX_HEADS * T, 1)
    kin_T = lane_pad(jnp.transpose(ki_s3, (0, 2, 1)))
    n_grp = N_HEADS // HEADS_PER_GROUP
    qg = jnp.transpose((q_s * HEAD_DIM ** -0.5).reshape(Bs, T, n_grp, HEADS_PER_GROUP, HEAD_DIM), (0, 2, 3, 1, 4))
    qbd = jnp.einsum('bghtd,hk->bghtkd', qg, jnp.eye(HEADS_PER_GROUP, dtype=F32))
    qbd = qbd.reshape(Bs, n_grp, HEADS_PER_GROUP * T, HEADS_PER_GROUP * HEAD_DIM).astype(BF16)
    kT_new = lane_pad(jnp.transpose(k_s, (0, 2, 3, 1)))
    vT_new = lane_pad(jnp.transpose(v_s, (0, 2, 3, 1)))
    n_pages = page_table.shape[1]
    topk_s = min(TOPK_MAX, (n_pages * PAGE_SIZE + T) // 4)
    keys = _sample_scores(page_table, qi_rows, wi_rows, kin_T, jnp.transpose(cache_k_idx[l], (0, 2, 1)))
    bias = _select_bias(keys.reshape(Ms, -1), topk_s).reshape(Bs, T, -1)
    att = _sample_attention(page_table, qbd, bias, kT_new, vT_new,
                            jnp.transpose(cache_k[l], (0, 2, 3, 1)), jnp.transpose(cache_v[l], (0, 2, 3, 1)))
    att = att.reshape(Bs, n_grp, HEADS_PER_GROUP, T, HEADS_PER_GROUP, HEAD_DIM)
    att = jnp.diagonal(att, axis1=2, axis2=4)
    ya_s = jnp.transpose(att, (0, 2, 1, 4, 3)).reshape(Ms, D_ATTN).astype(BF16)
    m_s = _merge(ycs, ya_s, w_bc[l], w_ba[l], zs[:, OFF_GCONV:], Ms, 512)
    tok = lambda v: jnp.broadcast_to(v, (Bs, T, D)).reshape(Ms, D)
    y_sample = _tail(x_sample.reshape(Ms, D), m_s, w_o[l], w_ff1[l], w_ff2[l], norm2_g[l], final_g,
                     tok(g1), tok(sc2), tok(sh2), tok(g2), Ms, Ms, None).reshape(Bs, T, D)

    st = lambda a_: a_[None]
    return (y_prompt, y_sample, st(k_prompt), st(v_prompt), st(kidx_prompt), st(cv_p),
            st(k_s), st(v_s), st(ki_s3), st(cv_s))
```

```python
import functools

import jax
import jax.numpy as jnp
import numpy as np
from jax import lax
from jax.experimental import pallas as pl
from jax.experimental.pallas import tpu as pltpu

D_MODEL = 2048
PAGE_SIZE = 128
D_CONV = 1024
CONV_W = 3
N_HEADS = 16
HEAD_DIM = 64
D_ATTN = N_HEADS * HEAD_DIM
N_IDX_HEADS = 16
IDX_DIM = 64
TOPK_MAX = 256
D_FF = 4 * D_MODEL
EPS = 1e-6
IN_SIZES = (D_CONV, D_CONV, D_CONV, D_ATTN, D_ATTN, D_ATTN,
            N_IDX_HEADS * IDX_DIM, IDX_DIM, N_IDX_HEADS, D_MODEL, D_MODEL)
IN_OFFS = tuple(int(v) for v in np.cumsum((0,) + IN_SIZES))
OFF_U, OFF_GB, OFF_GC, OFF_Q, OFF_K, OFF_V, OFF_QI, OFF_KI, OFF_WI, OFF_GCONV, OFF_GATTN, N_IN = IN_OFFS

BF16 = jnp.bfloat16
F32 = jnp.float32
I32 = jnp.int32
INT_MIN = -2 ** 31
NEG_BIG = -1e30
VMEM_LIMIT_BYTES = 56 * 1024 * 1024
ATTN_BLOCK = 256


def _cparams(sem):
    return pltpu.CompilerParams(dimension_semantics=sem, vmem_limit_bytes=VMEM_LIMIT_BYTES)


def _mm_kernel(a_ref, b_ref, *refs, n_extra, trans_b, prologue, epilogue):
    extra, outs = refs[:n_extra], refs[n_extra:]
    a = a_ref[...]
    if prologue is not None:
        a = prologue(a)
    dims = (((1,), (1,)), ((), ())) if trans_b else (((1,), (0,)), ((), ()))
    d = lax.dot_general(a.astype(BF16), b_ref[...].astype(BF16), dims, preferred_element_type=F32)
    for o, r in zip(outs, epilogue(d, *[e[...] for e in extra])):
        o[...] = r.astype(o.dtype)


def _mm(a, b, *, trans_b, tm, tn, a_row_off=0, extras=(), outs, epilogue, prologue=None, m=None, n=None):
    kdim = a.shape[1]
    m = a.shape[0] if m is None else m
    if n is None:
        n = b.shape[0] if trans_b else b.shape[1]
    b_spec = pl.BlockSpec((tn, kdim), lambda i, j: (j, 0)) if trans_b else pl.BlockSpec((kdim, tn), lambda i, j: (0, j))
    return pl.pallas_call(
        functools.partial(_mm_kernel, n_extra=len(extras), trans_b=trans_b, prologue=prologue, epilogue=epilogue),
        grid=(pl.cdiv(m, tm), pl.cdiv(n, tn)),
        in_specs=[pl.BlockSpec((tm, kdim), lambda i, j: (i + a_row_off, 0)), b_spec] + [s for _, s in extras],
        out_specs=[s for _, s in outs],
        out_shape=[o for o, _ in outs],
        compiler_params=_cparams(("parallel", "arbitrary")),
    )(a, b, *[e for e, _ in extras])


def _mm_rows_kernel(a_ref, b_ref, *refs, n_step, n_fin, step_fn, final_fn):
    steps, fins, outs = refs[:n_step], refs[n_step:n_step + n_fin], refs[n_step + n_fin:]
    j = pl.program_id(1)
    tn = b_ref.shape[1]
    d = jnp.dot(a_ref[...], b_ref[...].astype(BF16), preferred_element_type=F32)
    outs[0][:, pl.ds(pl.multiple_of(j * tn, tn), tn)] = step_fn(d, *[e[...] for e in steps])

    @pl.when(j == pl.num_programs(1) - 1)
    def _():
        for o, r in zip(outs, final_fn(outs[0][...], *[f[...] for f in fins])):
            if r is not None:
                o[...] = r.astype(o.dtype)


def _mm_rows(a, b, *, tm, tn, step_extras, fin_extras, out_dtypes, step_fn, final_fn):
    M, kdim = a.shape
    N = b.shape[1]
    row = pl.BlockSpec((tm, N), lambda i, j: (i, 0))
    return pl.pallas_call(
        functools.partial(_mm_rows_kernel, n_step=len(step_extras), n_fin=len(fin_extras),
                          step_fn=step_fn, final_fn=final_fn),
        grid=(M // tm, N // tn),
        in_specs=[pl.BlockSpec((tm, kdim), lambda i, j: (i, 0)), pl.BlockSpec((kdim, tn), lambda i, j: (0, j))]
        + [s for _, s in step_extras] + [s for _, s in fin_extras],
        out_specs=[row] * len(out_dtypes),
        out_shape=[jax.ShapeDtypeStruct((M, N), dt) for dt in out_dtypes],
        compiler_params=_cparams(("parallel", "arbitrary")),
    )(a, b, *[e for e, _ in step_extras], *[e for e, _ in fin_extras])


def _spec_mn(tm, tn):
    return pl.BlockSpec((tm, tn), lambda i, j: (i, j))


class _RowVec:
    def __init__(self, tm, rows_per_seq):
        self.tm, self.bps = tm, (None if rows_per_seq is None else rows_per_seq // tm)

    def cols(self, tn):
        if self.bps is None:
            return pl.BlockSpec((self.tm, tn), lambda i, j: (i, j))
        return pl.BlockSpec((None, 1, tn), lambda i, j: (i // self.bps, 0, j))

    def full(self, n):
        if self.bps is None:
            return pl.BlockSpec((self.tm, n), lambda i, j: (i, 0))
        return pl.BlockSpec((None, 1, n), lambda i, j: (i // self.bps, 0, 0))


def _norm_mod_kernel(x_ref, g_ref, sc_ref, sh_ref, o_ref):
    x = x_ref[...]
    r = lax.rsqrt(jnp.mean(x * x, axis=-1, keepdims=True) + EPS)
    o_ref[...] = ((x * r * g_ref[...]) * (1.0 + sc_ref[...]) + sh_ref[...]).astype(o_ref.dtype)


def _norm_mod(x, g, sc, sh, tm):
    n_seq, L, D = x.shape
    bps = L // tm
    return pl.pallas_call(
        _norm_mod_kernel,
        grid=(n_seq, bps),
        in_specs=[pl.BlockSpec((None, tm, D), lambda s, i: (s, i, 0)),
                  pl.BlockSpec((1, D), lambda s, i: (0, 0)),
                  pl.BlockSpec((None, 1, D), lambda s, i: (s, 0, 0)),
                  pl.BlockSpec((None, 1, D), lambda s, i: (s, 0, 0))],
        out_specs=pl.BlockSpec((tm, D), lambda s, i: (s * bps + i, 0)),
        out_shape=jax.ShapeDtypeStruct((n_seq * L, D), BF16),
        compiler_params=_cparams(("parallel", "arbitrary")),
    )(x, g.reshape(1, D), sc, sh)


def _conv_kernel(u_ref, gb_ref, gc_ref, prev_ref, w_ref, y_ref, new_ref, carry_ref):
    i = pl.program_id(1)
    tm = u_ref.shape[0]

    @pl.when(i == 0)
    def _():
        carry_ref[...] = prev_ref[...]

    cu = gc_ref[...] * u_ref[...]
    row = lax.broadcasted_iota(I32, cu.shape, 0)
    c0 = carry_ref[0:1, :]
    c1 = carry_ref[1:2, :]
    s1 = jnp.where(row == 0, c1, pltpu.roll(cu, 1, 0))
    s2 = jnp.where(row == 0, c0, jnp.where(row == 1, c1, pltpu.roll(cu, 2, 0)))
    conv = w_ref[0:1, :] * s2 + w_ref[1:2, :] * s1 + w_ref[2:3, :] * cu
    y_ref[...] = (gb_ref[...] * conv).astype(y_ref.dtype)
    last2 = cu[tm - 2:tm, :]
    carry_ref[...] = last2
    new_ref[...] = last2


def _gated_conv(z, prev, conv_w, n_seq, L, tm):
    C = D_CONV
    bps = L // tm
    row = lambda s, i: s * bps + i
    return pl.pallas_call(
        _conv_kernel,
        grid=(n_seq, bps),
        in_specs=[pl.BlockSpec((tm, C), lambda s, i: (row(s, i), 0)),
                  pl.BlockSpec((tm, C), lambda s, i: (row(s, i), 1)),
                  pl.BlockSpec((tm, C), lambda s, i: (row(s, i), 2)),
                  pl.BlockSpec((None, 2, C), lambda s, i: (s, 0, 0)),
                  pl.BlockSpec((CONV_W, C), lambda s, i: (0, 0))],
        out_specs=[pl.BlockSpec((tm, C), lambda s, i: (row(s, i), 0)),
                   pl.BlockSpec((None, 2, C), lambda s, i: (s, 0, 0))],
        out_shape=[jax.ShapeDtypeStruct((n_seq * L, C), BF16),
                   jax.ShapeDtypeStruct((n_seq, 2, C), F32)],
        scratch_shapes=[pltpu.VMEM((2, C), F32)],
        compiler_params=_cparams(("arbitrary", "arbitrary")),
    )(z, z, z, prev, conv_w)


def _merge_kernel(yc_ref, ya_ref, wbc_ref, wba_ref, gc_ref, ga_ref, o_ref):
    dc = jnp.dot(yc_ref[...], wbc_ref[...].astype(BF16), preferred_element_type=F32)
    da = jnp.dot(ya_ref[...], wba_ref[...].astype(BF16), preferred_element_type=F32)
    o_ref[...] = (jax.nn.sigmoid(gc_ref[...]) * dc + jax.nn.sigmoid(ga_ref[...]) * da).astype(o_ref.dtype)


def _merge(yc, ya, w_bc, w_ba, zg, tm, tn):
    M = yc.shape[0]
    nb = D_MODEL // tn
    return pl.pallas_call(
        _merge_kernel,
        grid=(M // tm, nb),
        in_specs=[pl.BlockSpec((tm, D_CONV), lambda i, j: (i, 0)),
                  pl.BlockSpec((tm, D_ATTN), lambda i, j: (i, 0)),
                  pl.BlockSpec((D_CONV, tn), lambda i, j: (0, j)),
                  pl.BlockSpec((D_ATTN, tn), lambda i, j: (0, j)),
                  pl.BlockSpec((tm, tn), lambda i, j: (i, j)),
                  pl.BlockSpec((tm, tn), lambda i, j: (i, j + nb))],
        out_specs=pl.BlockSpec((tm, tn), lambda i, j: (i, j)),
        out_shape=jax.ShapeDtypeStruct((M, D_MODEL), BF16),
        compiler_params=_cparams(("parallel", "arbitrary")),
    )(yc, ya, w_bc, w_ba, zg, zg)


NEG_INF = float("-inf")
LOWEST_KEY = INT_MIN + 0x00800000


def _key_to_float(key):
    return lax.bitcast_convert_type(key ^ ((key >> 31) & 0x7FFFFFFF), F32)


def _kth_largest(count, k, shape):
    n_ge = lambda key: count(_key_to_float(key), False)
    base = jnp.where(n_ge(jnp.zeros(shape, I32)) >= k, 0, INT_MIN).astype(I32)

    def bit_body(t, base):
        cand = base + jnp.left_shift(jnp.int32(1), 30 - t)
        return jnp.where(n_ge(cand) >= k, cand, base)

    return _key_to_float(jnp.maximum(lax.fori_loop(0, 31, bit_body, base), LOWEST_KEY))


def _prompt_attn_kernel(qT_ref, kT_ref, vT_ref, qiT_ref, ki_ref, wT_ref, o_ref,
                        key_scr, bias_scr, outT_scr, m_scr, l_scr, s_scr, *, topk):
    j = pl.program_id(1)
    tq = qT_ref.shape[1]
    ck = tq
    sub = ck // 2

    def score_body(c, _):
        k0 = pl.multiple_of(c * sub, sub)
        ki_c = ki_ref[pl.ds(k0, sub), :].astype(BF16)
        acc = jnp.zeros((sub, tq), F32)
        for h in range(N_IDX_HEADS):
            d = jnp.dot(ki_c, qiT_ref[h * IDX_DIM:(h + 1) * IDX_DIM, :], preferred_element_type=F32)
            acc = acc + jnp.maximum(d, 0.0) * wT_ref[h:h + 1, :]
        kpos = k0 + lax.broadcasted_iota(I32, (sub, tq), 0)
        qpos = j * tq + lax.broadcasted_iota(I32, (sub, tq), 1)
        key_scr[pl.ds(k0, sub), :] = jnp.where(kpos <= qpos, acc, NEG_INF)
        return 0

    lax.fori_loop(0, 2 * (j + 1), score_body, 0)

    def count(v, strict):
        def body(c, cnt):
            blk = key_scr[pl.ds(pl.multiple_of(c * ck, ck), ck), :]
            hit = (blk > v) if strict else (blk >= v)
            return cnt + jnp.sum(hit.astype(I32), axis=0, keepdims=True)
        return lax.fori_loop(0, j + 1, body, jnp.zeros((1, tq), I32))

    thr = _kth_largest(count, topk, (1, tq))
    n_ge = count(thr, False)
    n_gt = count(thr, True)
    need = topk - n_gt

    def bias_body(c, _):
        k0 = pl.multiple_of(c * ck, ck)
        bias_scr[pl.ds(k0, ck), :] = jnp.where(key_scr[pl.ds(k0, ck), :] >= thr, 0.0, NEG_BIG)
        return 0

    lax.fori_loop(0, j + 1, bias_body, 0)

    @pl.when(jnp.max(n_ge - n_gt - need) > 0)
    def _():
        tri = (lax.broadcasted_iota(I32, (ck, ck), 1) < lax.broadcasted_iota(I32, (ck, ck), 0)).astype(BF16)
        need_f = need.astype(F32)

        def tie_body(c, seen):
            k0 = pl.multiple_of(c * ck, ck)
            blk = key_scr[pl.ds(k0, ck), :]
            eq = blk == thr
            eq_f = jnp.where(eq, 1.0, 0.0)
            before = seen + jnp.dot(tri, eq_f.astype(BF16), preferred_element_type=F32)
            keep = (blk > thr) | (eq & (before < need_f))
            bias_scr[pl.ds(k0, ck), :] = jnp.where(keep, 0.0, NEG_BIG)
            return seen + jnp.sum(eq_f, axis=0, keepdims=True)

        lax.fori_loop(0, j + 1, tie_body, jnp.zeros((1, tq), F32))

    m_scr[...] = jnp.full(m_scr.shape, NEG_BIG, F32)
    l_scr[...] = jnp.zeros(l_scr.shape, F32)
    outT_scr[...] = jnp.zeros(outT_scr.shape, F32)

    def chunk_body(c, _):
        k0 = pl.multiple_of(c * ck, ck)

        def logits(h):
            r = slice(h * HEAD_DIM, (h + 1) * HEAD_DIM)
            return lax.dot_general(kT_ref[r, pl.ds(k0, ck)], qT_ref[r, :], (((0,), (0,)), ((), ())),
                                   preferred_element_type=F32)

        s_scr[0] = logits(0)
        for h in range(N_HEADS):
            r = slice(h * HEAD_DIM, (h + 1) * HEAD_DIM)
            if h + 1 < N_HEADS:
                s_scr[(h + 1) % 2] = logits(h + 1)
            s = s_scr[h % 2] + bias_scr[pl.ds(k0, ck), :]
            m_old = m_scr[h:h + 1, :]
            m_new = jnp.maximum(m_old, jnp.max(s, axis=0, keepdims=True))
            p = jnp.exp(s - m_new)
            alpha = jnp.exp(m_old - m_new)
            l_scr[h:h + 1, :] = l_scr[h:h + 1, :] * alpha + jnp.sum(p, axis=0, keepdims=True)
            pv = jnp.dot(vT_ref[r, pl.ds(k0, ck)], p.astype(BF16), preferred_element_type=F32)
            outT_scr[r, :] = outT_scr[r, :] * alpha + pv
            m_scr[h:h + 1, :] = m_new
        return 0

    lax.fori_loop(0, j + 1, chunk_body, 0)
    out = outT_scr[...].reshape(N_HEADS, HEAD_DIM, tq) / l_scr[...][:, None, :]
    o_ref[...] = out.reshape(N_HEADS * HEAD_DIM, tq).T.astype(o_ref.dtype)


def _prompt_attention(qT, kT, vT, qiT, ki, wT, topk):
    B, _, L = qT.shape
    tq = ATTN_BLOCK
    nq = L // tq
    feat = lambda b, j: (b, 0, j)
    whole = lambda b, j: (b, 0, 0)
    return pl.pallas_call(
        functools.partial(_prompt_attn_kernel, topk=topk),
        name="prompt_attn",
        grid=(B, nq),
        in_specs=[pl.BlockSpec((None, D_ATTN, tq), feat),
                  pl.BlockSpec((None, D_ATTN, L), whole),
                  pl.BlockSpec((None, D_ATTN, L), whole),
                  pl.BlockSpec((None, N_IDX_HEADS * IDX_DIM, tq), feat),
                  pl.BlockSpec((None, L, IDX_DIM), whole),
                  pl.BlockSpec((None, N_IDX_HEADS, tq), feat)],
        out_specs=pl.BlockSpec((tq, D_ATTN), lambda b, j: (b * nq + j, 0)),
        out_shape=jax.ShapeDtypeStruct((B * L, D_ATTN), BF16),
        scratch_shapes=[pltpu.VMEM((L, tq), F32), pltpu.VMEM((L, tq), F32),
                        pltpu.VMEM((D_ATTN, tq), F32),
                        pltpu.VMEM((N_HEADS, tq), F32), pltpu.VMEM((N_HEADS, tq), F32),
                        pltpu.VMEM((2, tq, tq), F32)],
        compiler_params=_cparams(("parallel", "arbitrary")),
    )(qT, kT, vT, qiT, ki, wT)


IDX_PAGES_PER_STEP = 32
ATT_PAGES_PER_STEP = 16
PAGES_PER_TILE = 2
HEADS_PER_GROUP = 4


def _sample_index_kernel(pt_ref, q_ref, w_ref, kin_ref, *rest, n_pages_step):
    del pt_ref
    page_refs = rest[:n_pages_step]
    key_ref = rest[n_pages_step]
    g = pl.program_id(1)
    ng = pl.num_programs(1)
    n_tok = key_ref.shape[0]
    q = q_ref[...]
    w = w_ref[...]

    def scores(kT):
        d = jnp.dot(q, kT.astype(BF16), preferred_element_type=F32)
        r = jnp.maximum(d, 0.0) * w
        return jnp.sum(r.reshape(N_IDX_HEADS, n_tok, r.shape[1]), axis=0)

    npt = PAGES_PER_TILE
    for c in range(0, n_pages_step, npt):
        col = pl.multiple_of((g * n_pages_step + c) * PAGE_SIZE, npt * PAGE_SIZE)
        kT = jnp.concatenate([page_refs[c + i][...] for i in range(npt)], axis=1)
        key_ref[:, pl.ds(col, npt * PAGE_SIZE)] = scores(kT)

    @pl.when(g == ng - 1)
    def _():
        n_past = key_ref.shape[1] - PAGE_SIZE
        t = lax.broadcasted_iota(I32, (n_tok, PAGE_SIZE), 0)
        jn = lax.broadcasted_iota(I32, (n_tok, PAGE_SIZE), 1)
        key_ref[:, n_past:] = jnp.where(jn <= t, scores(kin_ref[...]), NEG_INF)


def _sample_scores(page_table, q_rows, w_rows, kin_T, cache_kidx_T):
    Bs, n_pages = page_table.shape
    n = IDX_PAGES_PER_STEP
    T = q_rows.shape[1] // N_IDX_HEADS
    width = (n_pages + 1) * PAGE_SIZE
    per_seq = lambda b, g, pt: (b, 0, 0)
    page_specs = [pl.BlockSpec((None, IDX_DIM, PAGE_SIZE), lambda b, g, pt, c=c: (pt[b, g * n + c], 0, 0))
                  for c in range(n)]
    return pl.pallas_call(
        functools.partial(_sample_index_kernel, n_pages_step=n),
        name="sample_scores",
        grid_spec=pltpu.PrefetchScalarGridSpec(
            num_scalar_prefetch=1,
            grid=(Bs, n_pages // n),
            in_specs=[pl.BlockSpec((None,) + q_rows.shape[1:], per_seq),
                      pl.BlockSpec((None,) + w_rows.shape[1:], per_seq),
                      pl.BlockSpec((None, IDX_DIM, PAGE_SIZE), per_seq)] + page_specs,
            out_specs=pl.BlockSpec((None, T, width), per_seq)),
        out_shape=jax.ShapeDtypeStruct((Bs, T, width), F32),
        compiler_params=_cparams(("parallel", "arbitrary")),
    )(page_table, q_rows, w_rows, kin_T, *([cache_kidx_T] * n))


SELECT_ROWS = 64


def _select_bias_kernel(key_ref, bias_ref, *, topk):
    rows, width = key_ref.shape

    def count(v, strict):
        hit = (key_ref[...] > v) if strict else (key_ref[...] >= v)
        return jnp.sum(hit.astype(I32), axis=1, keepdims=True)

    thr = _kth_largest(count, topk, (rows, 1))
    n_ge = count(thr, False)
    n_gt = count(thr, True)
    need = topk - n_gt
    bias_ref[...] = jnp.where(key_ref[...] >= thr, 0.0, NEG_BIG)

    @pl.when(jnp.max(n_ge - n_gt - need) > 0)
    def _():
        w = PAGE_SIZE
        tri = (lax.broadcasted_iota(I32, (w, w), 0) < lax.broadcasted_iota(I32, (w, w), 1)).astype(BF16)
        need_f = need.astype(F32)

        def tie_body(c, seen):
            c0 = pl.multiple_of(c * w, w)
            blk = key_ref[:, pl.ds(c0, w)]
            eq = blk == thr
            eq_f = jnp.where(eq, 1.0, 0.0)
            before = seen + jnp.dot(eq_f.astype(BF16), tri, preferred_element_type=F32)
            keep = (blk > thr) | (eq & (before < need_f))
            bias_ref[:, pl.ds(c0, w)] = jnp.where(keep, 0.0, NEG_BIG)
            return seen + jnp.sum(eq_f, axis=1, keepdims=True)

        lax.fori_loop(0, width // w, tie_body, jnp.zeros((rows, 1), F32))


def _select_bias(keys, topk):
    n_rows, width = keys.shape
    spec = pl.BlockSpec((SELECT_ROWS, width), lambda i: (i, 0))
    return pl.pallas_call(
        functools.partial(_select_bias_kernel, topk=topk),
        name="select_bias",
        grid=(n_rows // SELECT_ROWS,),
        in_specs=[spec],
        out_specs=spec,
        out_shape=jax.ShapeDtypeStruct((n_rows, width), F32),
        compiler_params=_cparams(("parallel",)),
    )(keys)


def _sample_attn_kernel(pt_ref, qbd_ref, bias_ref, biasn_ref, kn_ref, vn_ref, *rest, n_pages_step):
    del pt_ref
    k_refs, v_refs = rest[:n_pages_step], rest[n_pages_step:2 * n_pages_step]
    o_ref, m_scr, l_scr, acc_scr = rest[2 * n_pages_step:]
    g = pl.program_id(1)
    ng = pl.num_programs(1)
    n_tok = bias_ref.shape[0]
    hg = HEADS_PER_GROUP
    rows_g = hg * n_tok

    @pl.when(g == 0)
    def _():
        m_scr[...] = jnp.full(m_scr.shape, NEG_BIG, F32)
        l_scr[...] = jnp.zeros(l_scr.shape, F32)
        acc_scr[...] = jnp.zeros(acc_scr.shape, F32)

    def process(tiles):
        n_grp = N_HEADS // hg

        def side_by_side(refs, grp):
            return jnp.concatenate(
                [r[grp * hg:(grp + 1) * hg].reshape(hg * HEAD_DIM, PAGE_SIZE).astype(BF16) for r in refs], axis=1)

        def logits(grp):
            qg = qbd_ref[grp]
            out = []
            for k_refs_t, _, b in tiles:
                s = jnp.dot(qg, side_by_side(k_refs_t, grp), preferred_element_type=F32)
                out.append((s.reshape(hg, n_tok, s.shape[1]) + b[None]).reshape(rows_g, s.shape[1]))
            return out

        s_next = logits(0)
        for grp in range(n_grp):
            rows = slice(grp * rows_g, (grp + 1) * rows_g)
            s_list = s_next
            if grp + 1 < n_grp:
                s_next = logits(grp + 1)
            m_old = m_scr[rows, :]
            m_new = jnp.maximum(m_old, jnp.max(functools.reduce(jnp.maximum, s_list), axis=1, keepdims=True))
            alpha = jnp.exp(m_old - m_new)
            psum = jnp.zeros(s_list[0].shape, F32)
            pv = jnp.zeros((rows_g, hg * HEAD_DIM), F32)
            for s, (_, v_refs_t, _) in zip(s_list, tiles):
                p = jnp.exp(s - m_new)
                psum = psum + p
                pv = pv + lax.dot_general(p.astype(BF16), side_by_side(v_refs_t, grp), (((1,), (1,)), ((), ())),
                                          preferred_element_type=F32)
            l_scr[rows, :] = l_scr[rows, :] * alpha + jnp.sum(psum, axis=1, keepdims=True)
            acc_scr[rows, :] = acc_scr[rows, :] * alpha + pv
            m_scr[rows, :] = m_new

    npt = PAGES_PER_TILE
    process([(k_refs[c:c + npt], v_refs[c:c + npt], bias_ref[:, c * PAGE_SIZE:(c + npt) * PAGE_SIZE])
             for c in range(0, n_pages_step, npt)])

    @pl.when(g == ng - 1)
    def _():
        process([((kn_ref,), (vn_ref,), biasn_ref[...])])
        o_ref[...] = acc_scr[...] / l_scr[...]


def _sample_attention(page_table, qbd, bias, kT_new, vT_new, cache_kT, cache_vT):
    Bs, n_pages = page_table.shape
    n = ATT_PAGES_PER_STEP
    T = bias.shape[1]
    rows = N_HEADS * T
    per_seq4 = lambda b, g, pt: (b, 0, 0, 0)
    page_block = (None, N_HEADS, HEAD_DIM, PAGE_SIZE)
    page_specs = [pl.BlockSpec(page_block, lambda b, g, pt, c=c: (pt[b, g * n + c], 0, 0, 0)) for c in range(n)]
    return pl.pallas_call(
        functools.partial(_sample_attn_kernel, n_pages_step=n),
        name="sample_attn",
        grid_spec=pltpu.PrefetchScalarGridSpec(
            num_scalar_prefetch=1,
            grid=(Bs, n_pages // n),
            in_specs=[pl.BlockSpec((None,) + qbd.shape[1:], per_seq4),
                      pl.BlockSpec((None, T, n * PAGE_SIZE), lambda b, g, pt: (b, 0, g)),
                      pl.BlockSpec((None, T, PAGE_SIZE), lambda b, g, pt: (b, 0, n_pages)),
                      pl.BlockSpec(page_block, per_seq4),
                      pl.BlockSpec(page_block, per_seq4)] + page_specs + page_specs,
            out_specs=pl.BlockSpec((None, rows, HEADS_PER_GROUP * HEAD_DIM), lambda b, g, pt: (b, 0, 0)),
            scratch_shapes=[pltpu.VMEM((rows, 1), F32), pltpu.VMEM((rows, 1), F32),
                            pltpu.VMEM((rows, HEADS_PER_GROUP * HEAD_DIM), F32)]),
        out_shape=jax.ShapeDtypeStruct((Bs, rows, HEADS_PER_GROUP * HEAD_DIM), F32),
        compiler_params=_cparams(("parallel", "arbitrary")),
    )(page_table, qbd, bias, bias, kT_new, vT_new, *([cache_kT] * n), *([cache_vT] * n))


def _silu(x):
    return x * jax.nn.sigmoid(x)


def _rms(x):
    return lax.rsqrt(jnp.mean(x * x, axis=-1, keepdims=True) + EPS)


def _adaln(c_all, w_ada, b_ada):
    n = c_all.shape[0]
    n_pad = -(-n // 8) * 8
    c_pad = jnp.pad(c_all, ((0, n_pad - n), (0, 0)))
    tn = 1536
    (mod,) = _mm(c_pad, w_ada, trans_b=False, tm=n_pad, tn=tn, prologue=_silu,
                 extras=[(b_ada.reshape(1, -1), pl.BlockSpec((1, tn), lambda i, j: (0, j)))],
                 outs=[(jax.ShapeDtypeStruct((n_pad, w_ada.shape[1]), F32), _spec_mn(n_pad, tn))],
                 epilogue=lambda acc, b: (acc + b,))
    return mod[:n]


def _ki_wi_epilogue(acc, g):
    ki = acc[:, :IDX_DIM]
    wi = acc[:, IDX_DIM:IDX_DIM + N_IDX_HEADS] * (N_IDX_HEADS ** -0.5 * IDX_DIM ** -0.5)
    return ki * _rms(ki) * g, wi


def _small_proj(h, wT_small, idx_k_g, tm):
    M = h.shape[0]
    n = IDX_DIM + N_IDX_HEADS
    return _mm(h, wT_small, trans_b=True, tm=tm, tn=n,
               extras=[(idx_k_g.reshape(1, IDX_DIM), pl.BlockSpec((1, IDX_DIM), lambda i, j: (0, 0)))],
               outs=[(jax.ShapeDtypeStruct((M, IDX_DIM), F32), pl.BlockSpec((tm, IDX_DIM), lambda i, j: (i, 0))),
                     (jax.ShapeDtypeStruct((M, N_IDX_HEADS), F32),
                      pl.BlockSpec((tm, N_IDX_HEADS), lambda i, j: (i, 0)))],
               epilogue=_ki_wi_epilogue)


def _proj_T(wT, row_off, n_rows, h, n_seq, L, scale, want_f32):
    tr, tc = 1024, 1024
    cps = L // tc
    spec = pl.BlockSpec((None, tr, tc), lambda i, j: (j // cps, i, j % cps))
    outs = [(jax.ShapeDtypeStruct((n_seq, n_rows, L), BF16), spec)]
    if want_f32:
        outs.append((jax.ShapeDtypeStruct((n_seq, n_rows, L), F32), spec))
    ep = (lambda acc: (acc * scale,) * len(outs)) if scale != 1.0 else (lambda acc: (acc,) * len(outs))
    return _mm(wT, h, trans_b=True, tm=tr, tn=tc, a_row_off=row_off // tr, m=n_rows, outs=outs, epilogue=ep)


FFN_CHUNK = 512


def _ffn_kernel(h2_ref, w1_ref, w2_ref, x1_ref, g2_ref, fg_ref, y_ref):
    f = pl.program_id(1)

    @pl.when(f == 0)
    def _():
        y_ref[...] = jnp.zeros(y_ref.shape, F32)

    a = jnp.dot(h2_ref[...], w1_ref[...].astype(BF16), preferred_element_type=F32)
    a = jnp.square(jnp.maximum(a, 0.0)).astype(BF16)
    for c0 in range(0, y_ref.shape[1], FFN_CHUNK):
        cols = slice(c0, c0 + FFN_CHUNK)
        y_ref[:, cols] += jnp.dot(a, w2_ref[:, cols].astype(BF16), preferred_element_type=F32)

    @pl.when(f == pl.num_programs(1) - 1)
    def _():
        x2 = x1_ref[...] + g2_ref[...] * y_ref[...]
        y_ref[...] = x2 * _rms(x2) * fg_ref[...]


def _tail(x_flat, m, w_o, w_ff1, w_ff2, norm2_g, final_g, g1, sc2, sh2, g2, tm_o, tm_f, rows_per_seq):
    M, D = x_flat.shape
    gvec = pl.BlockSpec((1, D), lambda i, j: (0, 0))
    tn_o = 512
    rv = _RowVec(tm_o, rows_per_seq)
    x1, h2 = _mm_rows(
        m, w_o, tm=tm_o, tn=tn_o,
        step_extras=[(x_flat, _spec_mn(tm_o, tn_o)), (g1, rv.cols(tn_o))],
        fin_extras=[(norm2_g.reshape(1, D), gvec), (sc2, rv.full(D)), (sh2, rv.full(D))],
        out_dtypes=[F32, BF16],
        step_fn=lambda d, x, g1_: x + g1_ * d,
        final_fn=lambda x1_, n2g, sc2_, sh2_: (None, (x1_ * _rms(x1_) * n2g) * (1.0 + sc2_) + sh2_))

    rv = _RowVec(tm_f, rows_per_seq)
    once = pl.Buffered(1)
    return pl.pallas_call(
        _ffn_kernel,
        name="ffn",
        grid=(M // tm_f, D_FF // FFN_CHUNK),
        in_specs=[pl.BlockSpec((tm_f, D), lambda i, f: (i, 0), pipeline_mode=once),
                  pl.BlockSpec((D, FFN_CHUNK), lambda i, f: (0, f)),
                  pl.BlockSpec((FFN_CHUNK, D), lambda i, f: (f, 0)),
                  pl.BlockSpec((tm_f, D), lambda i, f: (i, 0), pipeline_mode=once),
                  rv.full(D),
                  gvec],
        out_specs=pl.BlockSpec((tm_f, D), lambda i, f: (i, 0)),
        out_shape=jax.ShapeDtypeStruct((M, D), F32),
        compiler_params=_cparams(("parallel", "arbitrary")),
    )(h2, w_ff1, w_ff2, x1, g2, final_g.reshape(1, D))


def kernel(x_prompt, x_sample, cache_k, cache_v, cache_k_idx, state_conv, page_table,
           c_prompt, c_sample, w_ada, b_ada, norm1_g, norm2_g, w_in, conv_w, idx_k_g,
           w_bc, w_ba, w_o, w_ff1, w_ff2, final_g):
    B, L, D = x_prompt.shape
    Bs, T, _ = x_sample.shape
    Mp, Ms = B * L, Bs * T
    l = 0
    wT = jnp.transpose(w_in[l])
    wT_gates = wT[OFF_GCONV:]
    wT_small = wT[OFF_KI:OFF_GCONV]

    mod = _adaln(jnp.concatenate([c_prompt, c_sample], axis=0), w_ada[l], b_ada[l])
    mod_p = [v[:, None, :] for v in jnp.split(mod[:B], 6, axis=-1)]
    mod_s = [v[:, None, :] for v in jnp.split(mod[B:], 6, axis=-1)]

    sh1, sc1, g1, sh2, sc2, g2 = mod_p
    tm = 1024
    h = _norm_mod(x_prompt, norm1_g[l], sc1, sh1, tm)
    tm_proj = 2048
    (zA,) = _mm(h, wT, trans_b=True, tm=tm_proj, tn=512, n=3 * D_CONV,
                outs=[(jax.ShapeDtypeStruct((Mp, 3 * D_CONV), F32), _spec_mn(tm_proj, 512))],
                epilogue=lambda acc: (acc,))
    (zG,) = _mm(h, wT_gates, trans_b=True, tm=tm_proj, tn=512,
                outs=[(jax.ShapeDtypeStruct((Mp, 2 * D), F32), _spec_mn(tm_proj, 512))],
                epilogue=lambda acc: (acc,))
    ki_p, wi_p = _small_proj(h, wT_small, idx_k_g[l], tm)
    (qT,) = _proj_T(wT, OFF_Q, D_ATTN, h, B, L, HEAD_DIM ** -0.5, False)
    kT, kT32 = _proj_T(wT, OFF_K, D_ATTN, h, B, L, 1.0, True)
    vT, vT32 = _proj_T(wT, OFF_V, D_ATTN, h, B, L, 1.0, True)
    (qiT,) = _proj_T(wT, OFF_QI, N_IDX_HEADS * IDX_DIM, h, B, L, 1.0, False)

    conv0 = jnp.zeros((B, CONV_W - 1, D_CONV), F32)
    yc, cv_p = _gated_conv(zA, conv0, conv_w[l], B, L, tm)
    wT_p = jnp.transpose(wi_p.reshape(B, L, N_IDX_HEADS), (0, 2, 1))
    ya = _prompt_attention(qT, kT, vT, qiT, ki_p.reshape(B, L, IDX_DIM), wT_p, min(TOPK_MAX, L // 4))
    m = _merge(yc, ya, w_bc[l], w_ba[l], zG, tm, 1024)
    y_prompt = _tail(x_prompt.reshape(Mp, D), m, w_o[l], w_ff1[l], w_ff2[l], norm2_g[l], final_g,
                     g1, sc2, sh2, g2, 1024, 1024, L).reshape(B, L, D)
    k_prompt = jnp.transpose(kT32.reshape(B, N_HEADS, HEAD_DIM, L), (0, 3, 1, 2))
    v_prompt = jnp.transpose(vT32.reshape(B, N_HEADS, HEAD_DIM, L), (0, 3, 1, 2))
    kidx_prompt = ki_p.reshape(B, L, IDX_DIM)

    sh1, sc1, g1, sh2, sc2, g2 = mod_s
    hs = _norm_mod(x_sample, norm1_g[l], sc1, sh1, T)
    (zs,) = _mm(hs, wT, trans_b=True, tm=Ms, tn=512,
                outs=[(jax.ShapeDtypeStruct((Ms, N_IN), F32), _spec_mn(Ms, 512))],
                epilogue=lambda acc: (acc,))
    ki_s, wi_s = _small_proj(hs, wT_small, idx_k_g[l], Ms)
    ycs, cv_s = _gated_conv(zs, state_conv[l], conv_w[l], Bs, T, T)
    q_s = zs[:, OFF_Q:OFF_K].reshape(Bs, T, N_HEADS, HEAD_DIM)
    k_s = zs[:, OFF_K:OFF_V].reshape(Bs, T, N_HEADS, HEAD_DIM)
    v_s = zs[:, OFF_V:OFF_QI].reshape(Bs, T, N_HEADS, HEAD_DIM)
    qi_s = zs[:, OFF_QI:OFF_KI].reshape(Bs, T, N_IDX_HEADS, IDX_DIM)
    ki_s3 = ki_s.reshape(Bs, T, IDX_DIM)
    lane_pad = lambda a_: jnp.pad(a_, [(0, 0)] * (a_.ndim - 1) + [(0, PAGE_SIZE - T)])
    qi_rows = jnp.transpose(qi_s, (0, 2, 1, 3)).reshape(Bs, N_IDX_HEADS * T, IDX_DIM).astype(BF16)
    wi_rows = jnp.transpose(wi_s.reshape(Bs, T, N_IDX_HEADS), (0, 2, 1)).reshape(Bs, N_IDX_HEADS * T, 1)
    kin_T = lane_pad(jnp.transpose(ki_s3, (0, 2, 1)))
    n_grp = N_HEADS // HEADS_PER_GROUP
    qg = jnp.transpose((q_s * HEAD_DIM ** -0.5).reshape(Bs, T, n_grp, HEADS_PER_GROUP, HEAD_DIM), (0, 2, 3, 1, 4))
    qbd = jnp.einsum('bghtd,hk->bghtkd', qg, jnp.eye(HEADS_PER_GROUP, dtype=F32))
    qbd = qbd.reshape(Bs, n_grp, HEADS_PER_GROUP * T, HEADS_PER_GROUP * HEAD_DIM).astype(BF16)
    kT_new = lane_pad(jnp.transpose(k_s, (0, 2, 3, 1)))
    vT_new = lane_pad(jnp.transpose(v_s, (0, 2, 3, 1)))
    n_pages = page_table.shape[1]
    topk_s = min(TOPK_MAX, (n_pages * PAGE_SIZE + T) // 4)
    keys = _sample_scores(page_table, qi_rows, wi_rows, kin_T, jnp.transpose(cache_k_idx[l], (0, 2, 1)))
    bias = _select_bias(keys.reshape(Ms, -1), topk_s).reshape(Bs, T, -1)
    att = _sample_attention(page_table, qbd, bias, kT_new, vT_new,
                            jnp.transpose(cache_k[l], (0, 2, 3, 1)), jnp.transpose(cache_v[l], (0, 2, 3, 1)))
    att = att.reshape(Bs, n_grp, HEADS_PER_GROUP, T, HEADS_PER_GROUP, HEAD_DIM)
    att = jnp.diagonal(att, axis1=2, axis2=4)
    ya_s = jnp.transpose(att, (0, 2, 1, 4, 3)).reshape(Ms, D_ATTN).astype(BF16)
    m_s = _merge(ycs, ya_s, w_bc[l], w_ba[l], zs[:, OFF_GCONV:], Ms, 512)
    tok = lambda v: jnp.broadcast_to(v, (Bs, T, D)).reshape(Ms, D)
    y_sample = _tail(x_sample.reshape(Ms, D), m_s, w_o[l], w_ff1[l], w_ff2[l], norm2_g[l], final_g,
                     tok(g1), tok(sc2), tok(sh2), tok(g2), Ms, Ms, None).reshape(Bs, T, D)

    st = lambda a_: a_[None]
    return (y_prompt, y_sample, st(k_prompt), st(v_prompt), st(kidx_prompt), st(cv_p),
            st(k_s), st(v_s), st(ki_s3), st(cv_s))
```

```python
import functools

import jax
import jax.numpy as jnp
import numpy as np
from jax import lax
from jax.experimental import pallas as pl
from jax.experimental.pallas import tpu as pltpu

D_MODEL = 2048
PAGE_SIZE = 128
D_CONV = 1024
CONV_W = 3
N_HEADS = 16
HEAD_DIM = 64
D_ATTN = N_HEADS * HEAD_DIM
N_IDX_HEADS = 16
IDX_DIM = 64
TOPK_MAX = 256
D_FF = 4 * D_MODEL
EPS = 1e-6
IN_SIZES = (D_CONV, D_CONV, D_CONV, D_ATTN, D_ATTN, D_ATTN,
            N_IDX_HEADS * IDX_DIM, IDX_DIM, N_IDX_HEADS, D_MODEL, D_MODEL)
IN_OFFS = tuple(int(v) for v in np.cumsum((0,) + IN_SIZES))
OFF_U, OFF_GB, OFF_GC, OFF_Q, OFF_K, OFF_V, OFF_QI, OFF_KI, OFF_WI, OFF_GCONV, OFF_GATTN, N_IN = IN_OFFS

BF16 = jnp.bfloat16
F32 = jnp.float32
I32 = jnp.int32
INT_MIN = -2 ** 31
NEG_BIG = -1e30
Q_SCALE = HEAD_DIM ** -0.5 * 1.4426950408889634
VMEM_LIMIT_BYTES = 56 * 1024 * 1024
ATTN_BLOCK = 256


def _cparams(sem):
    return pltpu.CompilerParams(dimension_semantics=sem, vmem_limit_bytes=VMEM_LIMIT_BYTES)


def _mm_kernel(a_ref, b_ref, *refs, n_extra, trans_b, prologue, epilogue):
    extra, outs = refs[:n_extra], refs[n_extra:]
    a = a_ref[...]
    if prologue is not None:
        a = prologue(a)
    dims = (((1,), (1,)), ((), ())) if trans_b else (((1,), (0,)), ((), ()))
    d = lax.dot_general(a.astype(BF16), b_ref[...].astype(BF16), dims, preferred_element_type=F32)
    for o, r in zip(outs, epilogue(d, *[e[...] for e in extra])):
        o[...] = r.astype(o.dtype)


def _mm(a, b, *, trans_b, tm, tn, a_row_off=0, extras=(), outs, epilogue, prologue=None, m=None, n=None):
    kdim = a.shape[1]
    m = a.shape[0] if m is None else m
    if n is None:
        n = b.shape[0] if trans_b else b.shape[1]
    b_spec = pl.BlockSpec((tn, kdim), lambda i, j: (j, 0)) if trans_b else pl.BlockSpec((kdim, tn), lambda i, j: (0, j))
    return pl.pallas_call(
        functools.partial(_mm_kernel, n_extra=len(extras), trans_b=trans_b, prologue=prologue, epilogue=epilogue),
        grid=(pl.cdiv(m, tm), pl.cdiv(n, tn)),
        in_specs=[pl.BlockSpec((tm, kdim), lambda i, j: (i + a_row_off, 0)), b_spec] + [s for _, s in extras],
        out_specs=[s for _, s in outs],
        out_shape=[o for o, _ in outs],
        compiler_params=_cparams(("parallel", "arbitrary")),
    )(a, b, *[e for e, _ in extras])


def _mm_rows_kernel(a_ref, b_ref, *refs, n_step, n_fin, step_fn, final_fn):
    steps, fins, outs = refs[:n_step], refs[n_step:n_step + n_fin], refs[n_step + n_fin:]
    j = pl.program_id(1)
    tn = b_ref.shape[1]
    d = jnp.dot(a_ref[...], b_ref[...].astype(BF16), preferred_element_type=F32)
    outs[0][:, pl.ds(pl.multiple_of(j * tn, tn), tn)] = step_fn(d, *[e[...] for e in steps])

    @pl.when(j == pl.num_programs(1) - 1)
    def _():
        for o, r in zip(outs, final_fn(outs[0][...], *[f[...] for f in fins])):
            if r is not None:
                o[...] = r.astype(o.dtype)


def _mm_rows(a, b, *, tm, tn, step_extras, fin_extras, out_dtypes, step_fn, final_fn):
    M, kdim = a.shape
    N = b.shape[1]
    row = pl.BlockSpec((tm, N), lambda i, j: (i, 0))
    return pl.pallas_call(
        functools.partial(_mm_rows_kernel, n_step=len(step_extras), n_fin=len(fin_extras),
                          step_fn=step_fn, final_fn=final_fn),
        grid=(M // tm, N // tn),
        in_specs=[pl.BlockSpec((tm, kdim), lambda i, j: (i, 0)), pl.BlockSpec((kdim, tn), lambda i, j: (0, j))]
        + [s for _, s in step_extras] + [s for _, s in fin_extras],
        out_specs=[row] * len(out_dtypes),
        out_shape=[jax.ShapeDtypeStruct((M, N), dt) for dt in out_dtypes],
        compiler_params=_cparams(("parallel", "arbitrary")),
    )(a, b, *[e for e, _ in step_extras], *[e for e, _ in fin_extras])


def _spec_mn(tm, tn):
    return pl.BlockSpec((tm, tn), lambda i, j: (i, j))


class _RowVec:
    def __init__(self, tm, rows_per_seq):
        self.tm, self.bps = tm, (None if rows_per_seq is None else rows_per_seq // tm)

    def cols(self, tn):
        if self.bps is None:
            return pl.BlockSpec((self.tm, tn), lambda i, j: (i, j))
        return pl.BlockSpec((None, 1, tn), lambda i, j: (i // self.bps, 0, j))

    def full(self, n):
        if self.bps is None:
            return pl.BlockSpec((self.tm, n), lambda i, j: (i, 0))
        return pl.BlockSpec((None, 1, n), lambda i, j: (i // self.bps, 0, 0))


def _norm_mod_kernel(x_ref, g_ref, sc_ref, sh_ref, o_ref):
    x = x_ref[...]
    r = lax.rsqrt(jnp.mean(x * x, axis=-1, keepdims=True) + EPS)
    o_ref[...] = ((x * r * g_ref[...]) * (1.0 + sc_ref[...]) + sh_ref[...]).astype(o_ref.dtype)


def _norm_mod(x, g, sc, sh, tm):
    n_seq, L, D = x.shape
    bps = L // tm
    return pl.pallas_call(
        _norm_mod_kernel,
        grid=(n_seq, bps),
        in_specs=[pl.BlockSpec((None, tm, D), lambda s, i: (s, i, 0)),
                  pl.BlockSpec((1, D), lambda s, i: (0, 0)),
                  pl.BlockSpec((None, 1, D), lambda s, i: (s, 0, 0)),
                  pl.BlockSpec((None, 1, D), lambda s, i: (s, 0, 0))],
        out_specs=pl.BlockSpec((tm, D), lambda s, i: (s * bps + i, 0)),
        out_shape=jax.ShapeDtypeStruct((n_seq * L, D), BF16),
        compiler_params=_cparams(("parallel", "arbitrary")),
    )(x, g.reshape(1, D), sc, sh)


def _conv_kernel(u_ref, gb_ref, gc_ref, prev_ref, w_ref, y_ref, new_ref, carry_ref):
    i = pl.program_id(1)
    tm = u_ref.shape[0]

    @pl.when(i == 0)
    def _():
        carry_ref[...] = prev_ref[...]

    cu = gc_ref[...] * u_ref[...]
    row = lax.broadcasted_iota(I32, cu.shape, 0)
    c0 = carry_ref[0:1, :]
    c1 = carry_ref[1:2, :]
    s1 = jnp.where(row == 0, c1, pltpu.roll(cu, 1, 0))
    s2 = jnp.where(row == 0, c0, jnp.where(row == 1, c1, pltpu.roll(cu, 2, 0)))
    conv = w_ref[0:1, :] * s2 + w_ref[1:2, :] * s1 + w_ref[2:3, :] * cu
    y_ref[...] = (gb_ref[...] * conv).astype(y_ref.dtype)
    last2 = cu[tm - 2:tm, :]
    carry_ref[...] = last2
    new_ref[...] = last2


def _gated_conv(z, prev, conv_w, n_seq, L, tm):
    C = D_CONV
    bps = L // tm
    row = lambda s, i: s * bps + i
    return pl.pallas_call(
        _conv_kernel,
        grid=(n_seq, bps),
        in_specs=[pl.BlockSpec((tm, C), lambda s, i: (row(s, i), 0)),
                  pl.BlockSpec((tm, C), lambda s, i: (row(s, i), 1)),
                  pl.BlockSpec((tm, C), lambda s, i: (row(s, i), 2)),
                  pl.BlockSpec((None, 2, C), lambda s, i: (s, 0, 0)),
                  pl.BlockSpec((CONV_W, C), lambda s, i: (0, 0))],
        out_specs=[pl.BlockSpec((tm, C), lambda s, i: (row(s, i), 0)),
                   pl.BlockSpec((None, 2, C), lambda s, i: (s, 0, 0))],
        out_shape=[jax.ShapeDtypeStruct((n_seq * L, C), BF16),
                   jax.ShapeDtypeStruct((n_seq, 2, C), F32)],
        scratch_shapes=[pltpu.VMEM((2, C), F32)],
        compiler_params=_cparams(("arbitrary", "arbitrary")),
    )(z, z, z, prev, conv_w)


def _merge_kernel(yc_ref, ya_ref, wbc_ref, wba_ref, gc_ref, ga_ref, o_ref):
    dc = jnp.dot(yc_ref[...], wbc_ref[...].astype(BF16), preferred_element_type=F32)
    da = jnp.dot(ya_ref[...], wba_ref[...].astype(BF16), preferred_element_type=F32)
    o_ref[...] = (jax.nn.sigmoid(gc_ref[...]) * dc + jax.nn.sigmoid(ga_ref[...]) * da).astype(o_ref.dtype)


def _merge(yc, ya, w_bc, w_ba, zg, tm, tn):
    M = yc.shape[0]
    nb = D_MODEL // tn
    return pl.pallas_call(
        _merge_kernel,
        grid=(M // tm, nb),
        in_specs=[pl.BlockSpec((tm, D_CONV), lambda i, j: (i, 0)),
                  pl.BlockSpec((tm, D_ATTN), lambda i, j: (i, 0)),
                  pl.BlockSpec((D_CONV, tn), lambda i, j: (0, j)),
                  pl.BlockSpec((D_ATTN, tn), lambda i, j: (0, j)),
                  pl.BlockSpec((tm, tn), lambda i, j: (i, j)),
                  pl.BlockSpec((tm, tn), lambda i, j: (i, j + nb))],
        out_specs=pl.BlockSpec((tm, tn), lambda i, j: (i, j)),
        out_shape=jax.ShapeDtypeStruct((M, D_MODEL), BF16),
        compiler_params=_cparams(("parallel", "arbitrary")),
    )(yc, ya, w_bc, w_ba, zg, zg)


NEG_INF = float("-inf")
LOWEST_KEY = INT_MIN + 0x00800000


def _key_to_float(key):
    return lax.bitcast_convert_type(key ^ ((key >> 31) & 0x7FFFFFFF), F32)


def _kth_largest(count, k, shape):
    n_ge = lambda key: count(_key_to_float(key), False)
    base = jnp.where(n_ge(jnp.zeros(shape, I32)) >= k, 0, INT_MIN).astype(I32)

    def bit_body(t, base):
        cand = base + jnp.left_shift(jnp.int32(1), 30 - t)
        return jnp.where(n_ge(cand) >= k, cand, base)

    return _key_to_float(jnp.maximum(lax.fori_loop(0, 31, bit_body, base), LOWEST_KEY))


def _prompt_attn_kernel(qT_ref, kT_ref, vT_ref, qiT_ref, ki_ref, wT_ref, o_ref,
                        key_scr, bias_scr, outT_scr, m_scr, l_scr, s_scr, *, topk):
    j = pl.program_id(1)
    tq = qT_ref.shape[1]
    ck = tq
    sub = ck // 2

    def score_body(c, _):
        k0 = pl.multiple_of(c * sub, sub)
        ki_c = ki_ref[pl.ds(k0, sub), :].astype(BF16)
        acc = jnp.zeros((sub, tq), F32)
        for h in range(N_IDX_HEADS):
            d = jnp.dot(ki_c, qiT_ref[h * IDX_DIM:(h + 1) * IDX_DIM, :], preferred_element_type=F32)
            acc = acc + jnp.maximum(d, 0.0) * wT_ref[h:h + 1, :]
        kpos = k0 + lax.broadcasted_iota(I32, (sub, tq), 0)
        qpos = j * tq + lax.broadcasted_iota(I32, (sub, tq), 1)
        key_scr[pl.ds(k0, sub), :] = jnp.where(kpos <= qpos, acc, NEG_INF)
        return 0

    lax.fori_loop(0, 2 * (j + 1), score_body, 0)

    def count(v, strict):
        def body(c, cnt):
            blk = key_scr[pl.ds(pl.multiple_of(c * ck, ck), ck), :]
            hit = (blk > v) if strict else (blk >= v)
            return cnt + jnp.sum(hit.astype(I32), axis=0, keepdims=True)
        return lax.fori_loop(0, j + 1, body, jnp.zeros((1, tq), I32))

    thr = _kth_largest(count, topk, (1, tq))
    n_ge = count(thr, False)
    n_gt = count(thr, True)
    need = topk - n_gt

    def bias_body(c, _):
        k0 = pl.multiple_of(c * ck, ck)
        bias_scr[pl.ds(k0, ck), :] = jnp.where(key_scr[pl.ds(k0, ck), :] >= thr, 0.0, NEG_BIG)
        return 0

    lax.fori_loop(0, j + 1, bias_body, 0)

    @pl.when(jnp.max(n_ge - n_gt - need) > 0)
    def _():
        tri = (lax.broadcasted_iota(I32, (ck, ck), 1) < lax.broadcasted_iota(I32, (ck, ck), 0)).astype(BF16)
        need_f = need.astype(F32)

        def tie_body(c, seen):
            k0 = pl.multiple_of(c * ck, ck)
            blk = key_scr[pl.ds(k0, ck), :]
            eq = blk == thr
            eq_f = jnp.where(eq, 1.0, 0.0)
            before = seen + jnp.dot(tri, eq_f.astype(BF16), preferred_element_type=F32)
            keep = (blk > thr) | (eq & (before < need_f))
            bias_scr[pl.ds(k0, ck), :] = jnp.where(keep, 0.0, NEG_BIG)
            return seen + jnp.sum(eq_f, axis=0, keepdims=True)

        lax.fori_loop(0, j + 1, tie_body, jnp.zeros((1, tq), F32))

    m_scr[...] = jnp.full(m_scr.shape, NEG_BIG, F32)
    l_scr[...] = jnp.zeros(l_scr.shape, F32)
    outT_scr[...] = jnp.zeros(outT_scr.shape, F32)

    def chunk_body(c, _):
        k0 = pl.multiple_of(c * ck, ck)

        def logits(h):
            r = slice(h * HEAD_DIM, (h + 1) * HEAD_DIM)
            return lax.dot_general(kT_ref[r, pl.ds(k0, ck)], qT_ref[r, :], (((0,), (0,)), ((), ())),
                                   preferred_element_type=F32)

        s_scr[0] = logits(0)
        for h in range(N_HEADS):
            r = slice(h * HEAD_DIM, (h + 1) * HEAD_DIM)
            if h + 1 < N_HEADS:
                s_scr[(h + 1) % 2] = logits(h + 1)
            s = s_scr[h % 2] + bias_scr[pl.ds(k0, ck), :]
            m_old = m_scr[h:h + 1, :]
            m_new = jnp.maximum(m_old, jnp.max(s, axis=0, keepdims=True))
            p = jnp.exp2(s - m_new)
            alpha = jnp.exp2(m_old - m_new)
            l_scr[h:h + 1, :] = l_scr[h:h + 1, :] * alpha + jnp.sum(p, axis=0, keepdims=True)
            pv = jnp.dot(vT_ref[r, pl.ds(k0, ck)], p.astype(BF16), preferred_element_type=F32)
            outT_scr[r, :] = outT_scr[r, :] * alpha + pv
            m_scr[h:h + 1, :] = m_new
        return 0

    lax.fori_loop(0, j + 1, chunk_body, 0)
    out = outT_scr[...].reshape(N_HEADS, HEAD_DIM, tq) / l_scr[...][:, None, :]
    o_ref[...] = out.reshape(N_HEADS * HEAD_DIM, tq).T.astype(o_ref.dtype)


def _prompt_attention(qT, kT, vT, qiT, ki, wT, topk):
    B, _, L = qT.shape
    tq = ATTN_BLOCK
    nq = L // tq
    feat = lambda b, j: (b, 0, j)
    whole = lambda b, j: (b, 0, 0)
    return pl.pallas_call(
        functools.partial(_prompt_attn_kernel, topk=topk),
        name="prompt_attn",
        grid=(B, nq),
        in_specs=[pl.BlockSpec((None, D_ATTN, tq), feat),
                  pl.BlockSpec((None, D_ATTN, L), whole),
                  pl.BlockSpec((None, D_ATTN, L), whole),
                  pl.BlockSpec((None, N_IDX_HEADS * IDX_DIM, tq), feat),
                  pl.BlockSpec((None, L, IDX_DIM), whole),
                  pl.BlockSpec((None, N_IDX_HEADS, tq), feat)],
        out_specs=pl.BlockSpec((tq, D_ATTN), lambda b, j: (b * nq + j, 0)),
        out_shape=jax.ShapeDtypeStruct((B * L, D_ATTN), BF16),
        scratch_shapes=[pltpu.VMEM((L, tq), F32), pltpu.VMEM((L, tq), F32),
                        pltpu.VMEM((D_ATTN, tq), F32),
                        pltpu.VMEM((N_HEADS, tq), F32), pltpu.VMEM((N_HEADS, tq), F32),
                        pltpu.VMEM((2, tq, tq), F32)],
        compiler_params=_cparams(("parallel", "arbitrary")),
    )(qT, kT, vT, qiT, ki, wT)


IDX_PAGES_PER_STEP = 32
ATT_PAGES_PER_STEP = 16
PAGES_PER_TILE = 2
HEADS_PER_GROUP = 4


def _sample_index_kernel(pt_ref, q_ref, w_ref, kin_ref, *rest, n_pages_step):
    del pt_ref
    page_refs = rest[:n_pages_step]
    key_ref = rest[n_pages_step]
    g = pl.program_id(1)
    ng = pl.num_programs(1)
    n_tok = key_ref.shape[0]
    q = q_ref[...]
    w = w_ref[...]

    def scores(kT):
        d = jnp.dot(q, kT.astype(BF16), preferred_element_type=F32)
        r = jnp.maximum(d, 0.0) * w
        return jnp.sum(r.reshape(N_IDX_HEADS, n_tok, r.shape[1]), axis=0)

    npt = PAGES_PER_TILE
    for c in range(0, n_pages_step, npt):
        col = pl.multiple_of((g * n_pages_step + c) * PAGE_SIZE, npt * PAGE_SIZE)
        kT = jnp.concatenate([page_refs[c + i][...] for i in range(npt)], axis=1)
        key_ref[:, pl.ds(col, npt * PAGE_SIZE)] = scores(kT)

    @pl.when(g == ng - 1)
    def _():
        n_past = key_ref.shape[1] - PAGE_SIZE
        t = lax.broadcasted_iota(I32, (n_tok, PAGE_SIZE), 0)
        jn = lax.broadcasted_iota(I32, (n_tok, PAGE_SIZE), 1)
        key_ref[:, n_past:] = jnp.where(jn <= t, scores(kin_ref[...]), NEG_INF)


def _sample_scores(page_table, q_rows, w_rows, kin_T, cache_kidx_T):
    Bs, n_pages = page_table.shape
    n = IDX_PAGES_PER_STEP
    T = q_rows.shape[1] // N_IDX_HEADS
    width = (n_pages + 1) * PAGE_SIZE
    per_seq = lambda b, g, pt: (b, 0, 0)
    page_specs = [pl.BlockSpec((None, IDX_DIM, PAGE_SIZE), lambda b, g, pt, c=c: (pt[b, g * n + c], 0, 0))
                  for c in range(n)]
    return pl.pallas_call(
        functools.partial(_sample_index_kernel, n_pages_step=n),
        name="sample_scores",
        grid_spec=pltpu.PrefetchScalarGridSpec(
            num_scalar_prefetch=1,
            grid=(Bs, n_pages // n),
            in_specs=[pl.BlockSpec((None,) + q_rows.shape[1:], per_seq),
                      pl.BlockSpec((None,) + w_rows.shape[1:], per_seq),
                      pl.BlockSpec((None, IDX_DIM, PAGE_SIZE), per_seq)] + page_specs,
            out_specs=pl.BlockSpec((None, T, width), per_seq)),
        out_shape=jax.ShapeDtypeStruct((Bs, T, width), F32),
        compiler_params=_cparams(("parallel", "arbitrary")),
    )(page_table, q_rows, w_rows, kin_T, *([cache_kidx_T] * n))


SELECT_ROWS = 64


def _select_bias_kernel(key_ref, bias_ref, *, topk):
    rows, width = key_ref.shape

    def count(v, strict):
        hit = (key_ref[...] > v) if strict else (key_ref[...] >= v)
        return jnp.sum(hit.astype(I32), axis=1, keepdims=True)

    thr = _kth_largest(count, topk, (rows, 1))
    n_ge = count(thr, False)
    n_gt = count(thr, True)
    need = topk - n_gt
    bias_ref[...] = jnp.where(key_ref[...] >= thr, 0.0, NEG_BIG)

    @pl.when(jnp.max(n_ge - n_gt - need) > 0)
    def _():
        w = PAGE_SIZE
        tri = (lax.broadcasted_iota(I32, (w, w), 0) < lax.broadcasted_iota(I32, (w, w), 1)).astype(BF16)
        need_f = need.astype(F32)

        def tie_body(c, seen):
            c0 = pl.multiple_of(c * w, w)
            blk = key_ref[:, pl.ds(c0, w)]
            eq = blk == thr
            eq_f = jnp.where(eq, 1.0, 0.0)
            before = seen + jnp.dot(eq_f.astype(BF16), tri, preferred_element_type=F32)
            keep = (blk > thr) | (eq & (before < need_f))
            bias_ref[:, pl.ds(c0, w)] = jnp.where(keep, 0.0, NEG_BIG)
            return seen + jnp.sum(eq_f, axis=1, keepdims=True)

        lax.fori_loop(0, width // w, tie_body, jnp.zeros((rows, 1), F32))


def _select_bias(keys, topk):
    n_rows, width = keys.shape
    spec = pl.BlockSpec((SELECT_ROWS, width), lambda i: (i, 0))
    return pl.pallas_call(
        functools.partial(_select_bias_kernel, topk=topk),
        name="select_bias",
        grid=(n_rows // SELECT_ROWS,),
        in_specs=[spec],
        out_specs=spec,
        out_shape=jax.ShapeDtypeStruct((n_rows, width), F32),
        compiler_params=_cparams(("parallel",)),
    )(keys)


def _sample_attn_kernel(pt_ref, qbd_ref, bias_ref, biasn_ref, kn_ref, vn_ref, *rest, n_pages_step):
    del pt_ref
    k_refs, v_refs = rest[:n_pages_step], rest[n_pages_step:2 * n_pages_step]
    o_ref, m_scr, l_scr, acc_scr = rest[2 * n_pages_step:]
    g = pl.program_id(1)
    ng = pl.num_programs(1)
    n_tok = bias_ref.shape[0]
    hg = HEADS_PER_GROUP
    rows_g = hg * n_tok

    @pl.when(g == 0)
    def _():
        m_scr[...] = jnp.full(m_scr.shape, NEG_BIG, F32)
        l_scr[...] = jnp.zeros(l_scr.shape, F32)
        acc_scr[...] = jnp.zeros(acc_scr.shape, F32)

    def process(tiles):
        n_grp = N_HEADS // hg

        def side_by_side(refs, grp):
            return jnp.concatenate(
                [r[grp * hg:(grp + 1) * hg].reshape(hg * HEAD_DIM, PAGE_SIZE).astype(BF16) for r in refs], axis=1)

        def logits(grp):
            qg = qbd_ref[grp]
            out = []
            for k_refs_t, _, b in tiles:
                s = jnp.dot(qg, side_by_side(k_refs_t, grp), preferred_element_type=F32)
                out.append((s.reshape(hg, n_tok, s.shape[1]) + b[None]).reshape(rows_g, s.shape[1]))
            return out

        s_next = logits(0)
        for grp in range(n_grp):
            rows = slice(grp * rows_g, (grp + 1) * rows_g)
            s_list = s_next
            if grp + 1 < n_grp:
                s_next = logits(grp + 1)
            m_old = m_scr[rows, :]
            m_new = jnp.maximum(m_old, jnp.max(functools.reduce(jnp.maximum, s_list), axis=1, keepdims=True))
            alpha = jnp.exp2(m_old - m_new)
            psum = jnp.zeros(s_list[0].shape, F32)
            pv = jnp.zeros((rows_g, hg * HEAD_DIM), F32)
            for s, (_, v_refs_t, _) in zip(s_list, tiles):
                p = jnp.exp2(s - m_new)
                psum = psum + p
                pv = pv + lax.dot_general(p.astype(BF16), side_by_side(v_refs_t, grp), (((1,), (1,)), ((), ())),
                                          preferred_element_type=F32)
            l_scr[rows, :] = l_scr[rows, :] * alpha + jnp.sum(psum, axis=1, keepdims=True)
            acc_scr[rows, :] = acc_scr[rows, :] * alpha + pv
            m_scr[rows, :] = m_new

    npt = PAGES_PER_TILE
    process([(k_refs[c:c + npt], v_refs[c:c + npt], bias_ref[:, c * PAGE_SIZE:(c + npt) * PAGE_SIZE])
             for c in range(0, n_pages_step, npt)])

    @pl.when(g == ng - 1)
    def _():
        process([((kn_ref,), (vn_ref,), biasn_ref[...])])
        o_ref[...] = acc_scr[...] / l_scr[...]


def _sample_attention(page_table, qbd, bias, kT_new, vT_new, cache_kT, cache_vT):
    Bs, n_pages = page_table.shape
    n = ATT_PAGES_PER_STEP
    T = bias.shape[1]
    rows = N_HEADS * T
    per_seq4 = lambda b, g, pt: (b, 0, 0, 0)
    page_block = (None, N_HEADS, HEAD_DIM, PAGE_SIZE)
    page_specs = [pl.BlockSpec(page_block, lambda b, g, pt, c=c: (pt[b, g * n + c], 0, 0, 0)) for c in range(n)]
    return pl.pallas_call(
        functools.partial(_sample_attn_kernel, n_pages_step=n),
        name="sample_attn",
        grid_spec=pltpu.PrefetchScalarGridSpec(
            num_scalar_prefetch=1,
            grid=(Bs, n_pages // n),
            in_specs=[pl.BlockSpec((None,) + qbd.shape[1:], per_seq4),
                      pl.BlockSpec((None, T, n * PAGE_SIZE), lambda b, g, pt: (b, 0, g)),
                      pl.BlockSpec((None, T, PAGE_SIZE), lambda b, g, pt: (b, 0, n_pages)),
                      pl.BlockSpec(page_block, per_seq4),
                      pl.BlockSpec(page_block, per_seq4)] + page_specs + page_specs,
            out_specs=pl.BlockSpec((None, rows, HEADS_PER_GROUP * HEAD_DIM), lambda b, g, pt: (b, 0, 0)),
            scratch_shapes=[pltpu.VMEM((rows, 1), F32), pltpu.VMEM((rows, 1), F32),
                            pltpu.VMEM((rows, HEADS_PER_GROUP * HEAD_DIM), F32)]),
        out_shape=jax.ShapeDtypeStruct((Bs, rows, HEADS_PER_GROUP * HEAD_DIM), F32),
        compiler_params=_cparams(("parallel", "arbitrary")),
    )(page_table, qbd, bias, bias, kT_new, vT_new, *([cache_kT] * n), *([cache_vT] * n))


def _silu(x):
    return x * jax.nn.sigmoid(x)


def _rms(x):
    return lax.rsqrt(jnp.mean(x * x, axis=-1, keepdims=True) + EPS)


def _adaln(c_all, w_ada, b_ada):
    n = c_all.shape[0]
    n_pad = -(-n // 8) * 8
    c_pad = jnp.pad(c_all, ((0, n_pad - n), (0, 0)))
    tn = 1536
    (mod,) = _mm(c_pad, w_ada, trans_b=False, tm=n_pad, tn=tn, prologue=_silu,
                 extras=[(b_ada.reshape(1, -1), pl.BlockSpec((1, tn), lambda i, j: (0, j)))],
                 outs=[(jax.ShapeDtypeStruct((n_pad, w_ada.shape[1]), F32), _spec_mn(n_pad, tn))],
                 epilogue=lambda acc, b: (acc + b,))
    return mod[:n]


def _ki_wi_epilogue(acc, g):
    ki = acc[:, :IDX_DIM]
    wi = acc[:, IDX_DIM:IDX_DIM + N_IDX_HEADS] * (N_IDX_HEADS ** -0.5 * IDX_DIM ** -0.5)
    return ki * _rms(ki) * g, wi


def _small_proj(h, wT_small, idx_k_g, tm):
    M = h.shape[0]
    n = IDX_DIM + N_IDX_HEADS
    return _mm(h, wT_small, trans_b=True, tm=tm, tn=n,
               extras=[(idx_k_g.reshape(1, IDX_DIM), pl.BlockSpec((1, IDX_DIM), lambda i, j: (0, 0)))],
               outs=[(jax.ShapeDtypeStruct((M, IDX_DIM), F32), pl.BlockSpec((tm, IDX_DIM), lambda i, j: (i, 0))),
                     (jax.ShapeDtypeStruct((M, N_IDX_HEADS), F32),
                      pl.BlockSpec((tm, N_IDX_HEADS), lambda i, j: (i, 0)))],
               epilogue=_ki_wi_epilogue)


def _proj_T(wT, row_off, n_rows, h, n_seq, L, scale, want_f32):
    tr, tc = 1024, 1024
    cps = L // tc
    spec = pl.BlockSpec((None, tr, tc), lambda i, j: (j // cps, i, j % cps))
    outs = [(jax.ShapeDtypeStruct((n_seq, n_rows, L), BF16), spec)]
    if want_f32:
        outs.append((jax.ShapeDtypeStruct((n_seq, n_rows, L), F32), spec))
    ep = (lambda acc: (acc * scale,) * len(outs)) if scale != 1.0 else (lambda acc: (acc,) * len(outs))
    return _mm(wT, h, trans_b=True, tm=tr, tn=tc, a_row_off=row_off // tr, m=n_rows, outs=outs, epilogue=ep)


FFN_CHUNK = 512


def _ffn_kernel(h2_ref, w1_ref, w2_ref, x1_ref, g2_ref, fg_ref, y_ref):
    f = pl.program_id(1)

    @pl.when(f == 0)
    def _():
        y_ref[...] = jnp.zeros(y_ref.shape, F32)

    a = jnp.dot(h2_ref[...], w1_ref[...].astype(BF16), preferred_element_type=F32)
    a = jnp.square(jnp.maximum(a, 0.0)).astype(BF16)
    for c0 in range(0, y_ref.shape[1], FFN_CHUNK):
        cols = slice(c0, c0 + FFN_CHUNK)
        y_ref[:, cols] += jnp.dot(a, w2_ref[:, cols].astype(BF16), preferred_element_type=F32)

    @pl.when(f == pl.num_programs(1) - 1)
    def _():
        x2 = x1_ref[...] + g2_ref[...] * y_ref[...]
        y_ref[...] = x2 * _rms(x2) * fg_ref[...]


def _tail(x_flat, m, w_o, w_ff1, w_ff2, norm2_g, final_g, g1, sc2, sh2, g2, tm_o, tm_f, rows_per_seq):
    M, D = x_flat.shape
    gvec = pl.BlockSpec((1, D), lambda i, j: (0, 0))
    tn_o = 512
    rv = _RowVec(tm_o, rows_per_seq)
    x1, h2 = _mm_rows(
        m, w_o, tm=tm_o, tn=tn_o,
        step_extras=[(x_flat, _spec_mn(tm_o, tn_o)), (g1, rv.cols(tn_o))],
        fin_extras=[(norm2_g.reshape(1, D), gvec), (sc2, rv.full(D)), (sh2, rv.full(D))],
        out_dtypes=[F32, BF16],
        step_fn=lambda d, x, g1_: x + g1_ * d,
        final_fn=lambda x1_, n2g, sc2_, sh2_: (None, (x1_ * _rms(x1_) * n2g) * (1.0 + sc2_) + sh2_))

    rv = _RowVec(tm_f, rows_per_seq)
    once = pl.Buffered(1)
    return pl.pallas_call(
        _ffn_kernel,
        name="ffn",
        grid=(M // tm_f, D_FF // FFN_CHUNK),
        in_specs=[pl.BlockSpec((tm_f, D), lambda i, f: (i, 0), pipeline_mode=once),
                  pl.BlockSpec((D, FFN_CHUNK), lambda i, f: (0, f)),
                  pl.BlockSpec((FFN_CHUNK, D), lambda i, f: (f, 0)),
                  pl.BlockSpec((tm_f, D), lambda i, f: (i, 0), pipeline_mode=once),
                  rv.full(D),
                  gvec],
        out_specs=pl.BlockSpec((tm_f, D), lambda i, f: (i, 0)),
        out_shape=jax.ShapeDtypeStruct((M, D), F32),
        compiler_params=_cparams(("parallel", "arbitrary")),
    )(h2, w_ff1, w_ff2, x1, g2, final_g.reshape(1, D))


def kernel(x_prompt, x_sample, cache_k, cache_v, cache_k_idx, state_conv, page_table,
           c_prompt, c_sample, w_ada, b_ada, norm1_g, norm2_g, w_in, conv_w, idx_k_g,
           w_bc, w_ba, w_o, w_ff1, w_ff2, final_g):
    B, L, D = x_prompt.shape
    Bs, T, _ = x_sample.shape
    Mp, Ms = B * L, Bs * T
    l = 0
    wT = jnp.transpose(w_in[l])
    wT_gates = wT[OFF_GCONV:]
    wT_small = wT[OFF_KI:OFF_GCONV]

    mod = _adaln(jnp.concatenate([c_prompt, c_sample], axis=0), w_ada[l], b_ada[l])
    mod_p = [v[:, None, :] for v in jnp.split(mod[:B], 6, axis=-1)]
    mod_s = [v[:, None, :] for v in jnp.split(mod[B:], 6, axis=-1)]

    sh1, sc1, g1, sh2, sc2, g2 = mod_p
    tm = 1024
    h = _norm_mod(x_prompt, norm1_g[l], sc1, sh1, tm)
    tm_proj = 2048
    (zA,) = _mm(h, wT, trans_b=True, tm=tm_proj, tn=512, n=3 * D_CONV,
                outs=[(jax.ShapeDtypeStruct((Mp, 3 * D_CONV), F32), _spec_mn(tm_proj, 512))],
                epilogue=lambda acc: (acc,))
    (zG,) = _mm(h, wT_gates, trans_b=True, tm=tm_proj, tn=512,
                outs=[(jax.ShapeDtypeStruct((Mp, 2 * D), F32), _spec_mn(tm_proj, 512))],
                epilogue=lambda acc: (acc,))
    ki_p, wi_p = _small_proj(h, wT_small, idx_k_g[l], tm)
    (qT,) = _proj_T(wT, OFF_Q, D_ATTN, h, B, L, Q_SCALE, False)
    kT, kT32 = _proj_T(wT, OFF_K, D_ATTN, h, B, L, 1.0, True)
    vT, vT32 = _proj_T(wT, OFF_V, D_ATTN, h, B, L, 1.0, True)
    (qiT,) = _proj_T(wT, OFF_QI, N_IDX_HEADS * IDX_DIM, h, B, L, 1.0, False)

    conv0 = jnp.zeros((B, CONV_W - 1, D_CONV), F32)
    yc, cv_p = _gated_conv(zA, conv0, conv_w[l], B, L, tm)
    wT_p = jnp.transpose(wi_p.reshape(B, L, N_IDX_HEADS), (0, 2, 1))
    ya = _prompt_attention(qT, kT, vT, qiT, ki_p.reshape(B, L, IDX_DIM), wT_p, min(TOPK_MAX, L // 4))
    m = _merge(yc, ya, w_bc[l], w_ba[l], zG, tm, 1024)
    y_prompt = _tail(x_prompt.reshape(Mp, D), m, w_o[l], w_ff1[l], w_ff2[l], norm2_g[l], final_g,
                     g1, sc2, sh2, g2, 1024, 1024, L).reshape(B, L, D)
    k_prompt = jnp.transpose(kT32.reshape(B, N_HEADS, HEAD_DIM, L), (0, 3, 1, 2))
    v_prompt = jnp.transpose(vT32.reshape(B, N_HEADS, HEAD_DIM, L), (0, 3, 1, 2))
    kidx_prompt = ki_p.reshape(B, L, IDX_DIM)

    sh1, sc1, g1, sh2, sc2, g2 = mod_s
    hs = _norm_mod(x_sample, norm1_g[l], sc1, sh1, T)
    (zs,) = _mm(hs, wT, trans_b=True, tm=Ms, tn=512,
                outs=[(jax.ShapeDtypeStruct((Ms, N_IN), F32), _spec_mn(Ms, 512))],
                epilogue=lambda acc: (acc,))
    ki_s, wi_s = _small_proj(hs, wT_small, idx_k_g[l], Ms)
    ycs, cv_s = _gated_conv(zs, state_conv[l], conv_w[l], Bs, T, T)
    q_s = zs[:, OFF_Q:OFF_K].reshape(Bs, T, N_HEADS, HEAD_DIM)
    k_s = zs[:, OFF_K:OFF_V].reshape(Bs, T, N_HEADS, HEAD_DIM)
    v_s = zs[:, OFF_V:OFF_QI].reshape(Bs, T, N_HEADS, HEAD_DIM)
    qi_s = zs[:, OFF_QI:OFF_KI].reshape(Bs, T, N_IDX_HEADS, IDX_DIM)
    ki_s3 = ki_s.reshape(Bs, T, IDX_DIM)
    lane_pad = lambda a_: jnp.pad(a_, [(0, 0)] * (a_.ndim - 1) + [(0, PAGE_SIZE - T)])
    qi_rows = jnp.transpose(qi_s, (0, 2, 1, 3)).reshape(Bs, N_IDX_HEADS * T, IDX_DIM).astype(BF16)
    wi_rows = jnp.transpose(wi_s.reshape(Bs, T, N_IDX_HEADS), (0, 2, 1)).reshape(Bs, N_IDX_HEADS * T, 1)
    kin_T = lane_pad(jnp.transpose(ki_s3, (0, 2, 1)))
    n_grp = N_HEADS // HEADS_PER_GROUP
    qg = jnp.transpose((q_s * Q_SCALE).reshape(Bs, T, n_grp, HEADS_PER_GROUP, HEAD_DIM), (0, 2, 3, 1, 4))
    qbd = jnp.einsum('bghtd,hk->bghtkd', qg, jnp.eye(HEADS_PER_GROUP, dtype=F32))
    qbd = qbd.reshape(Bs, n_grp, HEADS_PER_GROUP * T, HEADS_PER_GROUP * HEAD_DIM).astype(BF16)
    kT_new = lane_pad(jnp.transpose(k_s, (0, 2, 3, 1)))
    vT_new = lane_pad(jnp.transpose(v_s, (0, 2, 3, 1)))
    n_pages = page_table.shape[1]
    topk_s = min(TOPK_MAX, (n_pages * PAGE_SIZE + T) // 4)
    keys = _sample_scores(page_table, qi_rows, wi_rows, kin_T, jnp.transpose(cache_k_idx[l], (0, 2, 1)))
    bias = _select_bias(keys.reshape(Ms, -1), topk_s).reshape(Bs, T, -1)
    att = _sample_attention(page_table, qbd, bias, kT_new, vT_new,
                            jnp.transpose(cache_k[l], (0, 2, 3, 1)), jnp.transpose(cache_v[l], (0, 2, 3, 1)))
    att = att.reshape(Bs, n_grp, HEADS_PER_GROUP, T, HEADS_PER_GROUP, HEAD_DIM)
    att = jnp.diagonal(att, axis1=2, axis2=4)
    ya_s = jnp.transpose(att, (0, 2, 1, 4, 3)).reshape(Ms, D_ATTN).astype(BF16)
    m_s = _merge(ycs, ya_s, w_bc[l], w_ba[l], zs[:, OFF_GCONV:], Ms, 512)
    tok = lambda v: jnp.broadcast_to(v, (Bs, T, D)).reshape(Ms, D)
    y_sample = _tail(x_sample.reshape(Ms, D), m_s, w_o[l], w_ff1[l], w_ff2[l], norm2_g[l], final_g,
                     tok(g1), tok(sc2), tok(sh2), tok(g2), Ms, Ms, None).reshape(Bs, T, D)

    st = lambda a_: a_[None]
    return (y_prompt, y_sample, st(k_prompt), st(v_prompt), st(kidx_prompt), st(cv_p),
            st(k_s), st(v_s), st(ki_s3), st(cv_s))
```
